```python
import jax, jax.numpy as jnp
from jax import lax
import numpy as np

D_MODEL = 1024
BATCH = 2
SEQ = 8192
DEPTH = 4

N_MIXERS = 3
N_SB = (DEPTH + 2) // 3
N_ML = (DEPTH + 1) // 3
N_MLA = DEPTH // 3

BLOCK_Q = 128
EPS = 1e-6

SB_HEADS = 16
SB_HEAD_DIM = D_MODEL // SB_HEADS

ML_HEADS = 8
ML_V_DIM = D_MODEL // ML_HEADS
ML_QK_DIM = ML_V_DIM // 2
ML_CHUNK = 64
ML_CONV = 4
ML_QK_COLS = 2 * ML_HEADS * ML_QK_DIM
ML_IN_COLS = ML_QK_COLS + 2 * D_MODEL + 2 * ML_HEADS

MLA_HEADS = 16
MLA_NOPE = 64
MLA_ROPE = 32
MLA_V = 64
MLA_Q_RANK = 384
MLA_KV_RANK = 256
MLA_IN_COLS = MLA_Q_RANK + MLA_KV_RANK + MLA_ROPE
ROPE_THETA = 10000.0

D_FF = 2816

kernel_name = 'interleaved_sb_mlstm_mla_macaron'


def rmsnorm(x, g):
    xf = x.astype(jnp.float32)
    y = xf * lax.rsqrt(jnp.mean(xf * xf, axis=-1, keepdims=True) + EPS)
    return (y * g.astype(jnp.float32)).astype(x.dtype)


def swiglu(x, wi, wo):
    gate, up = jnp.split(x @ wi, 2, axis=-1)
    return (jax.nn.silu(gate) * up) @ wo


def sweep_query_blocks(block_fn, q):
    B, H, S, d = q.shape
    nb = S // BLOCK_Q
    qb = q.reshape(B, H, nb, BLOCK_Q, d).transpose(2, 0, 1, 3, 4)
    starts = jnp.arange(nb, dtype=jnp.int32) * BLOCK_Q
    out = lax.map(lambda a: block_fn(a[0], a[1]), (qb, starts))
    return out.transpose(1, 2, 0, 3, 4).reshape(B, H, S, out.shape[-1])


def stick_breaking_attention(x, w_in, w_out):
    B, S, _ = x.shape
    q, k, v = jnp.split(x @ w_in, 3, axis=-1)
    to_heads = lambda t: t.reshape(B, S, SB_HEADS, SB_HEAD_DIM).transpose(0, 2, 1, 3)
    q, k, v = to_heads(q), to_heads(k), to_heads(v)
    scale = SB_HEAD_DIM ** -0.5
    key_pos = jnp.arange(S, dtype=jnp.int32)

    def block(qb, start):
        z = jnp.einsum('bhqd,bhkd->bhqk', qb, k, preferred_element_type=jnp.float32) * scale
        q_pos = start + jnp.arange(BLOCK_Q, dtype=jnp.int32)
        strict = key_pos[None, :] < q_pos[:, None]
        log_beta = jax.nn.log_sigmoid(z)
        log_fail = jnp.where(strict, jax.nn.log_sigmoid(-z), 0.0)
        between = lax.cumsum(log_fail, axis=3, reverse=True) - log_fail
        w = jnp.where(strict, jnp.exp(log_beta + between), 0.0)
        return jnp.einsum('bhqk,bhkd->bhqd', w.astype(v.dtype), v)

    o = sweep_query_blocks(block, q)
    return o.transpose(0, 2, 1, 3).reshape(B, S, D_MODEL) @ w_out


def causal_depthwise_conv(x, w):
    K, C = w.shape
    return lax.conv_general_dilated(x, w[:, None, :].astype(x.dtype), window_strides=(1,),
                                    padding=[(K - 1, 0)],
                                    dimension_numbers=('NWC', 'WIO', 'NWC'),
                                    feature_group_count=C)


def mlstm(x, w_in, conv_w, b_igate, b_fgate, head_gain, w_out):
    B, S, _ = x.shape
    H, dk, dv, L = ML_HEADS, ML_QK_DIM, ML_V_DIM, ML_CHUNK
    f32 = jnp.float32
    p = x @ w_in
    qk, v, o_pre, gates = jnp.split(p, [ML_QK_COLS, ML_QK_COLS + D_MODEL, ML_QK_COLS + 2 * D_MODEL], axis=-1)
    qk = jax.nn.silu(causal_depthwise_conv(qk, conv_w))
    q, k = jnp.split(qk, 2, axis=-1)
    i_pre, f_pre = jnp.split(gates, 2, axis=-1)
    log_i = (i_pre + b_igate).astype(f32)
    log_f = jax.nn.log_sigmoid((f_pre + b_fgate).astype(f32))
    nc = S // L

    def chunks(t, d):
        return t.astype(f32).reshape(B, nc, L, H, d).transpose(1, 0, 3, 2, 4)

    qc = chunks(q * (dk ** -0.5), dk)
    kc = chunks(k, dk)
    vc = chunks(v, dv)
    lic = log_i.reshape(B, nc, L, H).transpose(1, 0, 3, 2)
    lfc = log_f.reshape(B, nc, L, H).transpose(1, 0, 3, 2)
    causal = jnp.tril(jnp.ones((L, L), dtype=bool))

    def step(carry, inp):
        C, n, m = carry
        qb, kb, vb, li, lf = inp
        b = jnp.cumsum(lf, axis=-1)
        dmat = jnp.where(causal, b[..., :, None] - b[..., None, :] + li[..., None, :], -jnp.inf)
        inter = b + m[..., None]
        m_t = jnp.maximum(jnp.max(dmat, axis=-1), inter)
        wt = jnp.exp(dmat - m_t[..., None])
        a = jnp.exp(inter - m_t)
        s = jnp.einsum('bhtd,bhsd->bhts', qb, kb) * wt
        num = jnp.einsum('bhts,bhsv->bhtv', s, vb) + a[..., None] * jnp.einsum('bhtd,bhdv->bhtv', qb, C)
        den = jnp.sum(s, axis=-1) + a * jnp.einsum('bhtd,bhd->bht', qb, n)
        h = num / jnp.maximum(jnp.abs(den), jnp.exp(-m_t))[..., None]
        b_last = b[..., -1]
        g = b_last[..., None] - b + li
        m_new = jnp.maximum(b_last + m, jnp.max(g, axis=-1))
        decay = jnp.exp(b_last + m - m_new)
        wk = jnp.exp(g - m_new[..., None])
        C_new = decay[..., None, None] * C + jnp.einsum('bhs,bhsd,bhsv->bhdv', wk, kb, vb)
        n_new = decay[..., None] * n + jnp.einsum('bhs,bhsd->bhd', wk, kb)
        return (C_new, n_new, m_new), h

    init = (jnp.zeros((B, H, dk, dv), f32), jnp.zeros((B, H, dk), f32), jnp.zeros((B, H), f32))
    _, hs = lax.scan(step, init, (qc, kc, vc, lic, lfc))
    h = hs.transpose(1, 0, 3, 2, 4).reshape(B, S, H, dv)
    h = rmsnorm(h, head_gain).reshape(B, S, D_MODEL).astype(x.dtype)
    return (jax.nn.sigmoid(o_pre) * h) @ w_out


def apply_rope(t, positions):
    half = t.shape[-1] // 2
    inv = ROPE_THETA ** (-jnp.arange(half, dtype=jnp.float32) / half)
    ang = positions.astype(jnp.float32)[:, :, None, None] * inv
    cos, sin = jnp.cos(ang), jnp.sin(ang)
    t1, t2 = jnp.split(t.astype(jnp.float32), 2, axis=-1)
    return jnp.concatenate([t1 * cos - t2 * sin, t1 * sin + t2 * cos], axis=-1).astype(t.dtype)


def mla(x, positions, w_in, g_q, w_uq, g_kv, w_ukv, w_out):
    B, S, _ = x.shape
    H = MLA_HEADS
    c_q, c_kv, k_r = jnp.split(x @ w_in, [MLA_Q_RANK, MLA_Q_RANK + MLA_KV_RANK], axis=-1)
    qh = (rmsnorm(c_q, g_q) @ w_uq).reshape(B, S, H, MLA_NOPE + MLA_ROPE)
    q_nope, q_rope = jnp.split(qh, [MLA_NOPE], axis=-1)
    q = jnp.concatenate([q_nope, apply_rope(q_rope, positions)], axis=-1)
    kvh = (rmsnorm(c_kv, g_kv) @ w_ukv).reshape(B, S, H, MLA_NOPE + MLA_V)
    k_nope, v = jnp.split(kvh, [MLA_NOPE], axis=-1)
    k_rope = apply_rope(k_r[:, :, None, :], positions)
    k = jnp.concatenate([k_nope, jnp.broadcast_to(k_rope, (B, S, H, MLA_ROPE))], axis=-1)
    q, k, v = (t.transpose(0, 2, 1, 3) for t in (q, k, v))
    scale = (MLA_NOPE + MLA_ROPE) ** -0.5
    key_pos = jnp.arange(S, dtype=jnp.int32)

    def block(qb, start):
        s = jnp.einsum('bhqd,bhkd->bhqk', qb, k, preferred_element_type=jnp.float32) * scale
        q_pos = start + jnp.arange(BLOCK_Q, dtype=jnp.int32)
        s = jnp.where(key_pos[None, :] <= q_pos[:, None], s, -jnp.inf)
        p = jax.nn.softmax(s, axis=-1)
        return jnp.einsum('bhqk,bhkd->bhqd', p.astype(v.dtype), v)

    o = sweep_query_blocks(block, q)
    return o.transpose(0, 2, 1, 3).reshape(B, S, H * MLA_V) @ w_out


def setup_inputs(seed: int = 0) -> dict:
    key = jax.random.key(seed)
    ks = iter(jax.random.split(key, 32))
    D = D_MODEL

    def w(shape, fan_in):
        return jax.random.normal(next(ks), shape, jnp.float32) * (fan_in ** -0.5)

    def gain(shape):
        return 1.0 + 0.01 * jax.random.normal(next(ks), shape, jnp.float32)

    x = jax.random.normal(next(ks), (BATCH, SEQ, D), jnp.float32)
    positions = jnp.broadcast_to(jnp.arange(SEQ, dtype=jnp.int32)[None, :], (BATCH, SEQ))
    return {
        'x': x,
        'positions': positions,
        'ln_ffn1': gain((DEPTH, D)),
        'ffn1_wi': w((DEPTH, D, 2 * D_FF), D),
        'ffn1_wo': w((DEPTH, D_FF, D), D_FF),
        'ln_mix': gain((DEPTH, D)),
        'ln_ffn2': gain((DEPTH, D)),
        'ffn2_wi': w((DEPTH, D, 2 * D_FF), D),
        'ffn2_wo': w((DEPTH, D_FF, D), D_FF),
        'sb_w_in': w((N_SB, D, 3 * D), D),
        'sb_w_out': w((N_SB, D, D), D),
        'ml_w_in': w((N_ML, D, ML_IN_COLS), D),
        'ml_conv_w': w((N_ML, ML_CONV, ML_QK_COLS), ML_CONV),
        'ml_b_igate': 0.1 * jax.random.normal(next(ks), (N_ML, ML_HEADS), jnp.float32),
        'ml_b_fgate': jnp.linspace(3.0, 6.0, ML_HEADS, dtype=jnp.float32)[None, :]
                      + 0.1 * jax.random.normal(next(ks), (N_ML, ML_HEADS), jnp.float32),
        'ml_head_gain': gain((N_ML, ML_HEADS, ML_V_DIM)),
        'ml_w_out': w((N_ML, D, D), D),
        'mla_w_in': w((N_MLA, D, MLA_IN_COLS), D),
        'mla_g_q': gain((N_MLA, MLA_Q_RANK)),
        'mla_w_uq': w((N_MLA, MLA_Q_RANK, MLA_HEADS * (MLA_NOPE + MLA_ROPE)), MLA_Q_RANK),
        'mla_g_kv': gain((N_MLA, MLA_KV_RANK)),
        'mla_w_ukv': w((N_MLA, MLA_KV_RANK, MLA_HEADS * (MLA_NOPE + MLA_V)), MLA_KV_RANK),
        'mla_w_out': w((N_MLA, MLA_HEADS * MLA_V, D), MLA_HEADS * MLA_V),
        'ln_final': gain((D,)),
    }


def reference(x, positions, ln_ffn1, ffn1_wi, ffn1_wo, ln_mix, ln_ffn2, ffn2_wi, ffn2_wo,
              sb_w_in, sb_w_out,
              ml_w_in, ml_conv_w, ml_b_igate, ml_b_fgate, ml_head_gain, ml_w_out,
              mla_w_in, mla_g_q, mla_w_uq, mla_g_kv, mla_w_ukv, mla_w_out,
              ln_final):
    h = x
    for i in range(DEPTH):
        h = h + 0.5 * swiglu(rmsnorm(h, ln_ffn1[i]), ffn1_wi[i], ffn1_wo[i])
        u = rmsnorm(h, ln_mix[i])
        j = i // N_MIXERS
        kind = i % N_MIXERS
        if kind == 0:
            mix = stick_breaking_attention(u, sb_w_in[j], sb_w_out[j])
        elif kind == 1:
            mix = mlstm(u, ml_w_in[j], ml_conv_w[j], ml_b_igate[j], ml_b_fgate[j],
                        ml_head_gain[j], ml_w_out[j])
        else:
            mix = mla(u, positions, mla_w_in[j], mla_g_q[j], mla_w_uq[j], mla_g_kv[j],
                      mla_w_ukv[j], mla_w_out[j])
        h = h + mix
        h = h + 0.5 * swiglu(rmsnorm(h, ln_ffn2[i]), ffn2_wi[i], ffn2_wo[i])
    return rmsnorm(h, ln_final)
```

```python
import functools

import jax
import jax.numpy as jnp
from jax import lax
from jax.experimental import pallas as pl
from jax.experimental.pallas import tpu as pltpu

F32 = jnp.float32
BF16 = jnp.bfloat16

EPS = 1e-6
LANES = 128
VMEM_LIMIT = 48 * 1024 * 1024

SB_HEADS = 16
SB_HEAD_DIM = 64
ML_HEADS = 8
ML_QK_DIM = 64
ML_V_DIM = 128
ML_CONV = 4
MLA_HEADS = 16
MLA_NOPE = 64
MLA_ROPE = 32
MLA_V = 64
MLA_Q_RANK = 384
MLA_KV_RANK = 256
ROPE_THETA = 10000.0
N_MIXERS = 3

EXP_UNDERFLOW = -105.0


def _dot(a, b):
    return jnp.dot(a, b, preferred_element_type=F32)


def _dot_nt(a, b):
    return lax.dot_general(a, b, (((1,), (1,)), ((), ())), preferred_element_type=F32)


def _dot_tn(a, b):
    return lax.dot_general(a, b, (((0,), (0,)), ((), ())), preferred_element_type=F32)


def _split_dot(x, m):
    hi = x.astype(BF16)
    lo = (x - hi.astype(F32)).astype(BF16)
    return _dot(hi, m) + _dot(lo, m)


def _log_sigmoid(x):
    return jnp.minimum(x, 0.0) - jnp.log(1.0 + jnp.exp(-jnp.abs(x)))


def _params(*sem):
    return pltpu.CompilerParams(dimension_semantics=sem, vmem_limit_bytes=VMEM_LIMIT)


def _ffn_body(h_ref, g_ref, wg_ref, wu_ref, wo_ref, *rest, nf, final):
    if final:
        gf_ref, o_ref, xn_ref, acc_ref = rest
    else:
        o_ref, xn_ref, acc_ref = rest
    f = pl.program_id(1)

    @pl.when(f == 0)
    def _():
        x = h_ref[...]
        ms = jnp.mean(x * x, axis=-1, keepdims=True)
        xn_ref[...] = (x * lax.rsqrt(ms + EPS) * g_ref[...]).astype(BF16)

    xn = xn_ref[...]
    gate = _dot(xn, wg_ref[...])
    up = _dot(xn, wu_ref[...])
    act = (gate * jax.nn.sigmoid(gate) * up).astype(BF16)
    part = _dot(act, wo_ref[...])

    @pl.when(f == 0)
    def _():
        acc_ref[...] = part

    @pl.when(f > 0)
    def _():
        acc_ref[...] += part

    @pl.when(f == nf - 1)
    def _():
        y = h_ref[...] + 0.5 * acc_ref[...]
        if final:
            ms = jnp.mean(y * y, axis=-1, keepdims=True)
            y = y * lax.rsqrt(ms + EPS) * gf_ref[...]
        o_ref[...] = y


def _ffn(h, g, wi, wo, g_final=None, *, tm=1024, tf=256):
    t, d = h.shape
    dff = wo.shape[0]
    nf = dff // tf
    final = g_final is not None
    in_specs = [
        pl.BlockSpec((tm, d), lambda i, f: (i, 0)),
        pl.BlockSpec((1, d), lambda i, f: (0, 0)),
        pl.BlockSpec((d, tf), lambda i, f: (0, f)),
        pl.BlockSpec((d, tf), lambda i, f: (0, nf + f)),
        pl.BlockSpec((tf, d), lambda i, f: (f, 0)),
    ]
    args = [h, g.reshape(1, d), wi, wi, wo]
    if final:
        in_specs.append(pl.BlockSpec((1, d), lambda i, f: (0, 0)))
        args.append(g_final.reshape(1, d))
    return pl.pallas_call(
        functools.partial(_ffn_body, nf=nf, final=final),
        grid=(t // tm, nf),
        in_specs=in_specs,
        out_specs=pl.BlockSpec((tm, d), lambda i, f: (i, 0)),
        out_shape=jax.ShapeDtypeStruct((t, d), F32),
        scratch_shapes=[pltpu.VMEM((tm, d), BF16), pltpu.VMEM((tm, d), F32)],
        compiler_params=_params("parallel", "arbitrary"),
        name="ffn_final" if final else "ffn",
    )(*args)


def _norm_matmul_body(x_ref, g_ref, w_ref, o_ref, xn_ref):
    @pl.when(pl.program_id(1) == 0)
    def _():
        x = x_ref[...]
        ms = jnp.mean(x * x, axis=-1, keepdims=True)
        xn_ref[...] = (x * lax.rsqrt(ms + EPS) * g_ref[...]).astype(BF16)

    o_ref[...] = _dot(xn_ref[...], w_ref[...]).astype(o_ref.dtype)


def _norm_matmul(x, g, w, out_dtype, *, tm=512, tn=1024, name="norm_matmul"):
    t, k = x.shape
    n = w.shape[1]
    tn = min(tn, n)
    return pl.pallas_call(
        _norm_matmul_body,
        grid=(t // tm, n // tn),
        in_specs=[
            pl.BlockSpec((tm, k), lambda i, j: (i, 0)),
            pl.BlockSpec((1, k), lambda i, j: (0, 0)),
            pl.BlockSpec((k, tn), lambda i, j: (0, j)),
        ],
        out_specs=pl.BlockSpec((tm, tn), lambda i, j: (i, j)),
        out_shape=jax.ShapeDtypeStruct((t, n), out_dtype),
        scratch_shapes=[pltpu.VMEM((tm, k), BF16)],
        compiler_params=_params("parallel", "arbitrary"),
        name=name,
    )(x, g.reshape(1, k), w)


def _matmul_residual_body(a_ref, w_ref, h_ref, o_ref):
    o_ref[...] = h_ref[...] + _dot(a_ref[...], w_ref[...])


def _matmul_residual(a, w, h, *, tm=1024):
    t, k = a.shape
    n = w.shape[1]
    return pl.pallas_call(
        _matmul_residual_body,
        grid=(t // tm,),
        in_specs=[
            pl.BlockSpec((tm, k), lambda i: (i, 0)),
            pl.BlockSpec((k, n), lambda i: (0, 0)),
            pl.BlockSpec((tm, n), lambda i: (i, 0)),
        ],
        out_specs=pl.BlockSpec((tm, n), lambda i: (i, 0)),
        out_shape=jax.ShapeDtypeStruct((t, n), F32),
        compiler_params=_params("parallel"),
        name="out_proj_residual",
    )(a, w, h)


def _sb_attn_body(q_ref, k_ref, v_ref, o_ref, acc_ref, *, tq, tk):
    i = pl.program_id(2)
    lane = lax.broadcasted_iota(jnp.int32, (1, LANES), 1)
    q = q_ref[...]
    qh = [jnp.where(lane < SB_HEAD_DIM, q, jnp.zeros_like(q)),
          jnp.where(lane >= SB_HEAD_DIM, q, jnp.zeros_like(q))]
    acc_ref[...] = jnp.zeros_like(acc_ref)

    row = lax.broadcasted_iota(jnp.int32, (tq, tk), 0) + i * tq
    col = lax.broadcasted_iota(jnp.int32, (tq, tk), 1)
    later = (lax.broadcasted_iota(jnp.int32, (tk, tk), 0)
             > lax.broadcasted_iota(jnp.int32, (tk, tk), 1)).astype(BF16)

    def cond(carry):
        j, _, _, cmax = carry
        return jnp.logical_and(j >= 0, cmax > EXP_UNDERFLOW)

    def body(carry):
        j, c0, c1, _ = carry
        start = pl.multiple_of(j * tk, tk)
        kb = k_ref[pl.ds(start, tk), :]
        vb = v_ref[pl.ds(start, tk), :]
        strict = (col + j * tk) < row
        cs = [c0, c1]
        for hh in range(2):
            z = _dot_nt(qh[hh], kb)
            soft = jnp.log(1.0 + jnp.exp(-jnp.abs(z)))
            log_beta = jnp.minimum(z, 0.0) - soft
            log_fail = jnp.where(strict, log_beta - z, 0.0)
            between = _split_dot(log_fail, later) + cs[hh]
            w = jnp.where(strict, jnp.exp(log_beta + between), 0.0)
            acc_ref[hh] += _dot(w.astype(BF16), vb)
            cs[hh] = cs[hh] + jnp.sum(log_fail, axis=-1, keepdims=True)
        cmax = jnp.max(jnp.maximum(cs[0], cs[1]))
        return j - 1, cs[0], cs[1], cmax

    zero = jnp.zeros((tq, 1), F32)
    j_first = ((i + 1) * tq) // tk - 1
    lax.while_loop(cond, body, (j_first, zero, zero, jnp.float32(0.0)))
    o_ref[...] = jnp.where(lane < SB_HEAD_DIM, acc_ref[0], acc_ref[1]).astype(o_ref.dtype)


def _sb_attention(qkv, batch, seq, *, tq=256, tk=128):
    t = batch * seq
    pairs = SB_HEADS // 2
    nq = seq // tq
    return pl.pallas_call(
        functools.partial(_sb_attn_body, tq=tq, tk=tk),
        grid=(batch, pairs, nq),
        in_specs=[
            pl.BlockSpec((tq, LANES), lambda b, p, i: (b * nq + i, p)),
            pl.BlockSpec((seq, LANES), lambda b, p, i: (b, pairs + p)),
            pl.BlockSpec((seq, LANES), lambda b, p, i: (b, 2 * pairs + p)),
        ],
        out_specs=pl.BlockSpec((tq, LANES), lambda b, p, i: (b * nq + i, p)),
        out_shape=jax.ShapeDtypeStruct((t, pairs * LANES), BF16),
        scratch_shapes=[pltpu.VMEM((2, tq, LANES), F32)],
        compiler_params=_params("parallel", "parallel", "arbitrary"),
        name="sb_attention",
    )(qkv, qkv, qkv)


def _mlstm_body(qk_ref, v_ref, op_ref, gcol_ref, grow_ref, cw_ref, bcol_ref, brow_ref, hg_ref,
                o_ref, xs_ref, st_ref, m_ref, *, chunk):
    L = chunk
    c = pl.program_id(1)
    dqk = ML_HEADS * ML_QK_DIM

    @pl.when(c == 0)
    def _():
        xs_ref[0:8, :] = jnp.zeros((8, 2 * dqk), F32)
        st_ref[...] = jnp.zeros_like(st_ref)
        m_ref[...] = jnp.zeros_like(m_ref)

    xs_ref[8:L + 8, :] = qk_ref[...].astype(F32)
    y = cw_ref[0:1, :] * xs_ref[5:L + 5, :]
    for tap in range(1, ML_CONV):
        y = y + cw_ref[tap:tap + 1, :] * xs_ref[5 + tap:L + 5 + tap, :]
    xs_ref[0:8, :] = xs_ref[L:L + 8, :]
    y = y * jax.nn.sigmoid(y)
    q_all = (y[:, :dqk] * (ML_QK_DIM ** -0.5)).astype(BF16)
    k_all = y[:, dqk:].astype(BF16)

    gcol = gcol_ref[...] + bcol_ref[...]
    ti = lax.broadcasted_iota(jnp.int32, (L, L), 0)
    si = lax.broadcasted_iota(jnp.int32, (L, L), 1)
    causal = ti >= si
    tri = causal.astype(BF16)
    bcol = _split_dot_left(tri, _log_sigmoid(gcol))
    grow = grow_ref[0] + brow_ref[...]
    brow = _split_dot(_log_sigmoid(grow), (si >= ti).astype(BF16))

    lane = lax.broadcasted_iota(jnp.int32, (1, LANES), 1)
    srow = lax.broadcasted_iota(jnp.int32, (LANES, 1), 0)
    ones_v = jnp.ones((L, ML_V_DIM), F32)

    for p in range(ML_HEADS // 2):
        qp = q_all[:, p * LANES:(p + 1) * LANES]
        kp = k_all[:, p * LANES:(p + 1) * LANES]
        state = st_ref[p]
        state_b = state.astype(BF16)
        upd = []
        dec = []
        for hh in range(2):
            h = 2 * p + hh
            sel = (lane >= ML_QK_DIM) if hh else (lane < ML_QK_DIM)
            qm = jnp.where(sel, qp, jnp.zeros_like(qp))
            km = jnp.where(sel, kp, jnp.zeros_like(kp))
            vh = v_ref[:, h * ML_V_DIM:(h + 1) * ML_V_DIM]
            b_c = bcol[:, ML_HEADS + h:ML_HEADS + h + 1]
            li_c = gcol[:, h:h + 1]
            b_r = brow[ML_HEADS + h:ML_HEADS + h + 1, :]
            li_r = grow[h:h + 1, :]
            m_prev = m_ref[h:h + 1, 0:1]

            dmat = jnp.where(causal, b_c - b_r + li_r, -jnp.inf)
            inter = b_c + m_prev
            m_t = jnp.maximum(jnp.max(dmat, axis=-1, keepdims=True), inter)
            wt = jnp.exp(dmat - m_t)
            a = jnp.exp(inter - m_t)
            s = _dot_nt(qm, kp) * wt
            qc = _dot(qm, state_b)
            num = _dot(s.astype(BF16), vh) + a * qc[:, :ML_V_DIM]
            den = jnp.sum(s, axis=-1, keepdims=True) + a * qc[:, ML_V_DIM:]
            hout = num / jnp.maximum(jnp.abs(den), jnp.exp(-m_t))

            b_last = b_c[L - 1:L, :]
            g = b_last - b_c + li_c
            m_new = jnp.maximum(b_last + m_prev, jnp.max(g, axis=0, keepdims=True))
            dec.append(jnp.exp(b_last + m_prev - m_new))
            wk = jnp.exp(g - m_new)
            v_aug = jnp.concatenate([vh.astype(F32), ones_v], axis=1)
            upd.append(_dot_tn(km, (wk * v_aug).astype(BF16)))
            m_ref[h:h + 1, :] = jnp.broadcast_to(m_new, (1, LANES))

            ms = jnp.mean(hout * hout, axis=-1, keepdims=True)
            hn = hout * lax.rsqrt(ms + EPS) * hg_ref[:, h * ML_V_DIM:(h + 1) * ML_V_DIM]
            og = jax.nn.sigmoid(op_ref[:, h * ML_V_DIM:(h + 1) * ML_V_DIM].astype(F32))
            o_ref[:, h * ML_V_DIM:(h + 1) * ML_V_DIM] = (og * hn).astype(o_ref.dtype)

        decay = jnp.where(srow < ML_QK_DIM, dec[0], dec[1])
        st_ref[p] = decay * state + upd[0] + upd[1]


def _split_dot_left(m, x):
    hi = x.astype(BF16)
    lo = (x - hi.astype(F32)).astype(BF16)
    return _dot(m, hi) + _dot(m, lo)


def _mlstm(proj, gates, conv_w, b_i, b_f, head_gain, batch, seq, *, chunk=256):
    t = batch * seq
    d = ML_HEADS * ML_V_DIM
    nc = seq // chunk
    grow = gates[:, :2 * ML_HEADS].reshape(batch, seq, 2 * ML_HEADS).transpose(0, 2, 1)
    bias = jnp.concatenate([b_i, b_f]).astype(F32)
    bcol = jnp.zeros((1, LANES), F32).at[0, :2 * ML_HEADS].set(bias)
    brow = bias.reshape(2 * ML_HEADS, 1)
    return pl.pallas_call(
        functools.partial(_mlstm_body, chunk=chunk),
        grid=(batch, nc),
        in_specs=[
            pl.BlockSpec((chunk, d), lambda b, c: (b * nc + c, 0)),
            pl.BlockSpec((chunk, d), lambda b, c: (b * nc + c, 1)),
            pl.BlockSpec((chunk, d), lambda b, c: (b * nc + c, 2)),
            pl.BlockSpec((chunk, LANES), lambda b, c: (b * nc + c, 0)),
            pl.BlockSpec((1, 2 * ML_HEADS, chunk), lambda b, c: (b, 0, c)),
            pl.BlockSpec((ML_CONV, d), lambda b, c: (0, 0)),
            pl.BlockSpec((1, LANES), lambda b, c: (0, 0)),
            pl.BlockSpec((2 * ML_HEADS, 1), lambda b, c: (0, 0)),
            pl.BlockSpec((1, d), lambda b, c: (0, 0)),
        ],
        out_specs=pl.BlockSpec((chunk, d), lambda b, c: (b * nc + c, 0)),
        out_shape=jax.ShapeDtypeStruct((t, d), BF16),
        scratch_shapes=[
            pltpu.VMEM((chunk + 8, d), F32),
            pltpu.VMEM((ML_HEADS // 2, LANES, 2 * ML_V_DIM), F32),
            pltpu.VMEM((ML_HEADS, LANES), F32),
        ],
        compiler_params=_params("parallel", "arbitrary"),
        name="mlstm",
    )(proj, proj, proj, gates, grow, conv_w.astype(F32), bcol, brow, head_gain.reshape(1, d))


def _mla_up_body(c_ref, pos_ref, gq_ref, gkv_ref, wq1_ref, wq2_ref, wk_ref, wv_ref, inv_ref,
                 q_ref, k_ref, v_ref):
    qr, kvr = MLA_Q_RANK, MLA_KV_RANK
    cq = c_ref[:, 0:qr]
    ckv = c_ref[:, 512:512 + kvr]
    kr1 = c_ref[:, 768:896]
    kr2 = c_ref[:, 896:1024]
    cqn = (cq * lax.rsqrt(jnp.mean(cq * cq, axis=-1, keepdims=True) + EPS) * gq_ref[...]).astype(BF16)
    ckvn = (ckv * lax.rsqrt(jnp.mean(ckv * ckv, axis=-1, keepdims=True) + EPS) * gkv_ref[...]).astype(BF16)

    ang = pos_ref[...].astype(F32) * inv_ref[...]
    cosv = jnp.cos(ang)
    sinv = jnp.sin(ang)
    lane = lax.broadcasted_iota(jnp.int32, (1, LANES), 1)
    half = MLA_ROPE // 2
    cpat = jnp.where(lane < MLA_NOPE, 1.0, jnp.where(lane < MLA_NOPE + MLA_ROPE, cosv, 0.0))
    spat = jnp.where(lane < MLA_NOPE, 0.0,
                     jnp.where(lane < MLA_NOPE + half, -sinv,
                               jnp.where(lane < MLA_NOPE + MLA_ROPE, sinv, 0.0)))
    scale = (MLA_NOPE + MLA_ROPE) ** -0.5
    cpat_q = cpat * scale
    spat_q = spat * scale

    a1 = _dot(cqn, wq1_ref[...])
    a2 = _dot(cqn, wq2_ref[...])
    kn = _dot(ckvn, wk_ref[...])
    rk = kr1 * cpat + kr2 * spat
    for h in range(MLA_HEADS):
        sl = slice(h * LANES, (h + 1) * LANES)
        q_ref[:, sl] = (a1[:, sl] * cpat_q + a2[:, sl] * spat_q).astype(BF16)
        k_ref[:, sl] = (kn[:, sl] + rk).astype(BF16)
    v_ref[...] = _dot(ckvn, wv_ref[...]).astype(BF16)


def _mla_up(cproj, pos, g_q, g_kv, wq1, wq2, wk, wv, inv_pat, *, tm=512):
    t = cproj.shape[0]
    hw = MLA_HEADS * LANES
    full = lambda shape: pl.BlockSpec(shape, lambda i: (0, 0))
    return pl.pallas_call(
        _mla_up_body,
        grid=(t // tm,),
        in_specs=[
            pl.BlockSpec((tm, cproj.shape[1]), lambda i: (i, 0)),
            pl.BlockSpec((tm, 1), lambda i: (i, 0)),
            full((1, MLA_Q_RANK)), full((1, MLA_KV_RANK)),
            full(wq1.shape), full(wq2.shape), full(wk.shape), full(wv.shape),
            full((1, LANES)),
        ],
        out_specs=[
            pl.BlockSpec((tm, hw), lambda i: (i, 0)),
            pl.BlockSpec((tm, hw), lambda i: (i, 0)),
            pl.BlockSpec((tm, MLA_HEADS * MLA_V), lambda i: (i, 0)),
        ],
        out_shape=[
            jax.ShapeDtypeStruct((t, hw), BF16),
            jax.ShapeDtypeStruct((t, hw), BF16),
            jax.ShapeDtypeStruct((t, MLA_HEADS * MLA_V), BF16),
        ],
        compiler_params=_params("parallel"),
        name="mla_up",
    )(cproj, pos, g_q.reshape(1, -1), g_kv.reshape(1, -1), wq1, wq2, wk, wv, inv_pat)


def _mla_attn_body(q_ref, k_ref, v_ref, o_ref, acc_ref, *, tq):
    i = pl.program_id(2)
    tk = tq
    lane = lax.broadcasted_iota(jnp.int32, (1, LANES), 1)
    acc_ref[...] = jnp.zeros_like(acc_ref)
    qs = [q_ref[:, 0:LANES], q_ref[:, LANES:2 * LANES]]

    def block(j, stats, masked):
        start = pl.multiple_of(j * tk, tk)
        vb = v_ref[pl.ds(start, tk), :]
        out = []
        for hh in range(2):
            m_prev, l_prev = stats[2 * hh], stats[2 * hh + 1]
            kb = k_ref[pl.ds(start, tk), hh * LANES:(hh + 1) * LANES]
            s = _dot_nt(qs[hh], kb)
            if masked:
                r = lax.broadcasted_iota(jnp.int32, (tq, tk), 0)
                cc = lax.broadcasted_iota(jnp.int32, (tq, tk), 1)
                s = jnp.where(cc <= r, s, -jnp.inf)
            m_new = jnp.maximum(m_prev, jnp.max(s, axis=-1, keepdims=True))
            alpha = jnp.exp(m_prev - m_new)
            pr = jnp.exp(s - m_new)
            l_new = alpha * l_prev + jnp.sum(pr, axis=-1, keepdims=True)
            acc_ref[hh] = alpha * acc_ref[hh] + _dot(pr.astype(BF16), vb)
            out += [m_new, l_new]
        return tuple(out)

    neg = jnp.full((tq, 1), -jnp.inf, F32)
    zero = jnp.zeros((tq, 1), F32)
    stats = lax.fori_loop(0, i, lambda j, st: block(j, st, False), (neg, zero, neg, zero))
    stats = block(i, stats, True)
    o0 = acc_ref[0] / stats[1]
    o1 = acc_ref[1] / stats[3]
    o_ref[...] = jnp.where(lane < MLA_V, o0, o1).astype(o_ref.dtype)


def _mla_attention(q, k, v, batch, seq, *, tq=256):
    t = batch * seq
    pairs = MLA_HEADS // 2
    nq = seq // tq
    return pl.pallas_call(
        functools.partial(_mla_attn_body, tq=tq),
        grid=(batch, pairs, nq),
        in_specs=[
            pl.BlockSpec((tq, 2 * LANES), lambda b, p, i: (b * nq + i, p)),
            pl.BlockSpec((seq, 2 * LANES), lambda b, p, i: (b, p)),
            pl.BlockSpec((seq, LANES), lambda b, p, i: (b, p)),
        ],
        out_specs=pl.BlockSpec((tq, LANES), lambda b, p, i: (b * nq + i, p)),
        out_shape=jax.ShapeDtypeStruct((t, pairs * LANES), BF16),
        scratch_shapes=[pltpu.VMEM((2, tq, LANES), F32)],
        compiler_params=_params("parallel", "parallel", "arbitrary"),
        name="mla_attention",
    )(q, k, v)


def _sb_mixer(h, g, w_in, w_out, batch, seq):
    d = h.shape[1]
    scale = SB_HEAD_DIM ** -0.5
    w = jnp.concatenate([w_in[:, :d] * scale, w_in[:, d:]], axis=1).astype(BF16)
    qkv = _norm_matmul(h, g, w, BF16, name="sb_in_proj")
    o = _sb_attention(qkv, batch, seq)
    return _matmul_residual(o, w_out.astype(BF16), h)


def _mlstm_mixer(h, g, w_in, conv_w, b_i, b_f, head_gain, w_out, batch, seq):
    d = h.shape[1]
    w_main = w_in[:, :3 * d].astype(BF16)
    w_gate = jnp.zeros((d, LANES), F32).at[:, :2 * ML_HEADS].set(w_in[:, 3 * d:]).astype(BF16)
    proj = _norm_matmul(h, g, w_main, BF16, name="ml_in_proj")
    gates = _norm_matmul(h, g, w_gate, F32, name="ml_gate_proj")
    o = _mlstm(proj, gates, conv_w, b_i, b_f, head_gain, batch, seq)
    return _matmul_residual(o, w_out.astype(BF16), h)


def _mla_weights(w_in, w_uq, w_ukv):
    d = w_in.shape[0]
    qr, kvr, r = MLA_Q_RANK, MLA_KV_RANK, MLA_ROPE
    half = r // 2
    w_kr = w_in[:, qr + kvr:]
    w_kr_swap = jnp.concatenate([w_kr[:, half:], w_kr[:, :half]], axis=1)
    place = lambda wr: jnp.zeros((d, LANES), F32).at[:, MLA_NOPE:MLA_NOPE + r].set(wr)
    w_c = jnp.concatenate([
        w_in[:, :qr], jnp.zeros((d, 512 - qr), F32),
        w_in[:, qr:qr + kvr], place(w_kr), place(w_kr_swap)], axis=1)

    wq = w_uq.reshape(qr, MLA_HEADS, MLA_NOPE + r)
    q_rope = wq[:, :, MLA_NOPE:]
    q_swap = jnp.concatenate([q_rope[:, :, half:], q_rope[:, :, :half]], axis=2)
    pad = jnp.zeros((qr, MLA_HEADS, LANES - MLA_NOPE - r), F32)
    wq1 = jnp.concatenate([wq, pad], axis=2).reshape(qr, MLA_HEADS * LANES)
    wq2 = jnp.concatenate([jnp.zeros((qr, MLA_HEADS, MLA_NOPE), F32), q_swap, pad], axis=2)
    wq2 = wq2.reshape(qr, MLA_HEADS * LANES)

    wkv = w_ukv.reshape(kvr, MLA_HEADS, MLA_NOPE + MLA_V)
    wk = jnp.concatenate([wkv[:, :, :MLA_NOPE], jnp.zeros((kvr, MLA_HEADS, LANES - MLA_NOPE), F32)], axis=2)
    wk = wk.reshape(kvr, MLA_HEADS * LANES)
    wv = wkv[:, :, MLA_NOPE:].reshape(kvr, MLA_HEADS * MLA_V)
    return w_c.astype(BF16), wq1.astype(BF16), wq2.astype(BF16), wk.astype(BF16), wv.astype(BF16)


def _mla_mixer(h, g, positions, w_in, g_q, w_uq, g_kv, w_ukv, w_out, batch, seq):
    w_c, wq1, wq2, wk, wv = _mla_weights(w_in, w_uq, w_ukv)
    cproj = _norm_matmul(h, g, w_c, F32, name="mla_in_proj")
    half = MLA_ROPE // 2
    inv = ROPE_THETA ** (-jnp.arange(half, dtype=F32) / half)
    inv_pat = jnp.zeros((1, LANES), F32).at[0, MLA_NOPE:MLA_NOPE + MLA_ROPE].set(jnp.concatenate([inv, inv]))
    pos = positions.reshape(batch * seq, 1)
    q, k, v = _mla_up(cproj, pos, g_q, g_kv, wq1, wq2, wk, wv, inv_pat)
    o = _mla_attention(q, k, v, batch, seq)
    return _matmul_residual(o, w_out.astype(BF16), h)


def kernel(x, positions, ln_ffn1, ffn1_wi, ffn1_wo, ln_mix, ln_ffn2, ffn2_wi, ffn2_wo, sb_w_in, sb_w_out, ml_w_in, ml_conv_w, ml_b_igate, ml_b_fgate, ml_head_gain, ml_w_out, mla_w_in, mla_g_q, mla_w_uq, mla_g_kv, mla_w_ukv, mla_w_out, ln_final):
    batch, seq, d = x.shape
    depth = ln_mix.shape[0]
    h = x.reshape(batch * seq, d)
    for i in range(depth):
        h = _ffn(h, ln_ffn1[i], ffn1_wi[i].astype(BF16), ffn1_wo[i].astype(BF16))
        j = i // N_MIXERS
        kind = i % N_MIXERS
        if kind == 0:
            h = _sb_mixer(h, ln_mix[i], sb_w_in[j], sb_w_out[j], batch, seq)
        elif kind == 1:
            h = _mlstm_mixer(h, ln_mix[i], ml_w_in[j], ml_conv_w[j], ml_b_igate[j], ml_b_fgate[j],
                             ml_head_gain[j], ml_w_out[j], batch, seq)
        else:
            h = _mla_mixer(h, ln_mix[i], positions, mla_w_in[j], mla_g_q[j], mla_w_uq[j], mla_g_kv[j],
                           mla_w_ukv[j], mla_w_out[j], batch, seq)
        g_final = ln_final if i == depth - 1 else None
        h = _ffn(h, ln_ffn2[i], ffn2_wi[i].astype(BF16), ffn2_wo[i].astype(BF16), g_final)
    return h.reshape(batch, seq, d)
```

```python
import functools

import jax
import jax.numpy as jnp
from jax import lax
from jax.experimental import pallas as pl
from jax.experimental.pallas import tpu as pltpu

F32 = jnp.float32
BF16 = jnp.bfloat16

EPS = 1e-6
LANES = 128
VMEM_LIMIT = 48 * 1024 * 1024

SB_HEADS = 16
SB_HEAD_DIM = 64
ML_HEADS = 8
ML_QK_DIM = 64
ML_V_DIM = 128
ML_CONV = 4
MLA_HEADS = 16
MLA_NOPE = 64
MLA_ROPE = 32
MLA_V = 64
MLA_Q_RANK = 384
MLA_KV_RANK = 256
ROPE_THETA = 10000.0
LOG2_E = 1.4426950408889634
N_MIXERS = 3

EXP2_UNDERFLOW = -152.0


def _dot(a, b):
    return jnp.dot(a, b, preferred_element_type=F32)


def _dot_nt(a, b):
    return lax.dot_general(a, b, (((1,), (1,)), ((), ())), preferred_element_type=F32)


def _dot_tn(a, b):
    return lax.dot_general(a, b, (((0,), (0,)), ((), ())), preferred_element_type=F32)


def _split_dot(x, m):
    hi = x.astype(BF16)
    lo = (x - hi.astype(F32)).astype(BF16)
    return _dot(hi, m) + _dot(lo, m)


def _log_sigmoid(x):
    return jnp.minimum(x, 0.0) - jnp.log(1.0 + jnp.exp(-jnp.abs(x)))


def _params(*sem):
    return pltpu.CompilerParams(dimension_semantics=sem, vmem_limit_bytes=VMEM_LIMIT)


def _ffn_body(h_ref, g_ref, wg_ref, wu_ref, wo_ref, *rest, nf, final):
    if final:
        gf_ref, o_ref, xn_ref, acc_ref = rest
    else:
        o_ref, xn_ref, acc_ref = rest
    f = pl.program_id(1)

    @pl.when(f == 0)
    def _():
        x = h_ref[...]
        ms = jnp.mean(x * x, axis=-1, keepdims=True)
        xn_ref[...] = (x * lax.rsqrt(ms + EPS) * g_ref[...]).astype(BF16)
        acc_ref[...] = jnp.zeros_like(acc_ref)

    xn = xn_ref[...]
    gate = _dot(xn, wg_ref[...])
    up = _dot(xn, wu_ref[...])
    act = (gate * jax.nn.sigmoid(gate) * up).astype(BF16)
    acc_ref[...] += _dot(act, wo_ref[...])

    @pl.when(f == nf - 1)
    def _():
        y = h_ref[...] + 0.5 * acc_ref[...]
        if final:
            ms = jnp.mean(y * y, axis=-1, keepdims=True)
            y = y * lax.rsqrt(ms + EPS) * gf_ref[...]
        o_ref[...] = y


def _ffn(h, g, wi, wo, g_final=None, *, tm=1024, tf=256):
    t, d = h.shape
    dff = wo.shape[0]
    nf = dff // tf
    final = g_final is not None
    in_specs = [
        pl.BlockSpec((tm, d), lambda i, f: (i, 0)),
        pl.BlockSpec((1, d), lambda i, f: (0, 0)),
        pl.BlockSpec((d, tf), lambda i, f: (0, f)),
        pl.BlockSpec((d, tf), lambda i, f: (0, nf + f)),
        pl.BlockSpec((tf, d), lambda i, f: (f, 0)),
    ]
    args = [h, g.reshape(1, d), wi, wi, wo]
    if final:
        in_specs.append(pl.BlockSpec((1, d), lambda i, f: (0, 0)))
        args.append(g_final.reshape(1, d))
    return pl.pallas_call(
        functools.partial(_ffn_body, nf=nf, final=final),
        grid=(t // tm, nf),
        in_specs=in_specs,
        out_specs=pl.BlockSpec((tm, d), lambda i, f: (i, 0)),
        out_shape=jax.ShapeDtypeStruct((t, d), F32),
        scratch_shapes=[pltpu.VMEM((tm, d), BF16), pltpu.VMEM((tm, d), F32)],
        compiler_params=_params("parallel", "arbitrary"),
        name="ffn_final" if final else "ffn",
    )(*args)


def _norm_matmul_body(x_ref, g_ref, w_ref, o_ref, xn_ref):
    @pl.when(pl.program_id(1) == 0)
    def _():
        x = x_ref[...]
        ms = jnp.mean(x * x, axis=-1, keepdims=True)
        xn_ref[...] = (x * lax.rsqrt(ms + EPS) * g_ref[...]).astype(BF16)

    o_ref[...] = _dot(xn_ref[...], w_ref[...]).astype(o_ref.dtype)


def _norm_matmul(x, g, w, out_dtype, *, tm=512, tn=1024, name="norm_matmul"):
    t, k = x.shape
    n = w.shape[1]
    tn = min(tn, n)
    return pl.pallas_call(
        _norm_matmul_body,
        grid=(t // tm, n // tn),
        in_specs=[
            pl.BlockSpec((tm, k), lambda i, j: (i, 0)),
            pl.BlockSpec((1, k), lambda i, j: (0, 0)),
            pl.BlockSpec((k, tn), lambda i, j: (0, j)),
        ],
        out_specs=pl.BlockSpec((tm, tn), lambda i, j: (i, j)),
        out_shape=jax.ShapeDtypeStruct((t, n), out_dtype),
        scratch_shapes=[pltpu.VMEM((tm, k), BF16)],
        compiler_params=_params("parallel", "arbitrary"),
        name=name,
    )(x, g.reshape(1, k), w)


def _matmul_residual_body(a_ref, w_ref, h_ref, o_ref):
    o_ref[...] = h_ref[...] + _dot(a_ref[...], w_ref[...])


def _matmul_residual(a, w, h, *, tm=1024):
    t, k = a.shape
    n = w.shape[1]
    return pl.pallas_call(
        _matmul_residual_body,
        grid=(t // tm,),
        in_specs=[
            pl.BlockSpec((tm, k), lambda i: (i, 0)),
            pl.BlockSpec((k, n), lambda i: (0, 0)),
            pl.BlockSpec((tm, n), lambda i: (i, 0)),
        ],
        out_specs=pl.BlockSpec((tm, n), lambda i: (i, 0)),
        out_shape=jax.ShapeDtypeStruct((t, n), F32),
        compiler_params=_params("parallel"),
        name="out_proj_residual",
    )(a, w, h)


def _sb_attn_body(q_ref, k_ref, v_ref, o_ref, acc_ref, *, tq, hp):
    i = pl.program_id(2)
    tk = tq
    nh = 2 * hp
    lane = lax.broadcasted_iota(jnp.int32, (1, LANES), 1)
    qs = []
    for p in range(hp):
        qp = q_ref[:, p * LANES:(p + 1) * LANES]
        qs.append(jnp.concatenate([jnp.where(lane < SB_HEAD_DIM, qp, jnp.zeros_like(qp)),
                                   jnp.where(lane >= SB_HEAD_DIM, qp, jnp.zeros_like(qp))], axis=0))
    later = (lax.broadcasted_iota(jnp.int32, (tk, tk), 0)
             > lax.broadcasted_iota(jnp.int32, (tk, tk), 1)).astype(BF16)
    later2 = jnp.concatenate([later, later], axis=0)

    def step(j, c, diagonal):
        start = pl.multiple_of(j * tk, tk)
        z = jnp.concatenate(
            [_dot_nt(qs[p], k_ref[pl.ds(start, tk), p * LANES:(p + 1) * LANES]) for p in range(hp)],
            axis=0)
        nz = jnp.minimum(z, 0.0)
        pz = z - nz
        soft = jnp.log(1.0 + jnp.exp2(nz - pz)) * LOG2_E
        log_beta = nz - soft
        log_fail = log_beta - z
        if diagonal:
            qrow = lax.broadcasted_iota(jnp.int32, (nh * tq, tk), 0) & (tq - 1)
            strict = lax.broadcasted_iota(jnp.int32, (nh * tq, tk), 1) < qrow
            log_fail = jnp.where(strict, log_fail, 0.0)
            log_beta = jnp.where(strict, log_beta, -jnp.inf)
        hi = log_fail.astype(BF16)
        lo = (log_fail - hi.astype(F32)).astype(BF16)
        between = _dot(jnp.concatenate([hi, lo], axis=1), later2) + c
        w = jnp.exp2(log_beta + between).astype(BF16)
        for p in range(hp):
            pv = _dot(w[2 * p * tq:2 * (p + 1) * tq], v_ref[pl.ds(start, tk), p * LANES:(p + 1) * LANES])
            if diagonal:
                acc_ref[p] = pv
            else:
                acc_ref[p] += pv
        c = c + jnp.sum(log_fail, axis=-1, keepdims=True)
        return c, jnp.max(c)

    c, cmax = step(i, jnp.zeros((nh * tq, 1), F32), True)

    def cond(carry):
        return jnp.logical_and(carry[0] >= 0, carry[1] > EXP2_UNDERFLOW)

    def body(carry):
        c, cmax = step(carry[0], carry[2], False)
        return carry[0] - 1, cmax, c

    lax.while_loop(cond, body, (i - 1, cmax, c))
    for p in range(hp):
        o_ref[:, p * LANES:(p + 1) * LANES] = jnp.where(
            lane < SB_HEAD_DIM, acc_ref[p, 0:tq], acc_ref[p, tq:2 * tq]).astype(o_ref.dtype)


def _sb_attention(qkv, batch, seq, *, tq=256, hp=2):
    t = batch * seq
    groups = SB_HEADS // 2 // hp
    nq = seq // tq
    w = hp * LANES
    return pl.pallas_call(
        functools.partial(_sb_attn_body, tq=tq, hp=hp),
        grid=(batch, groups, nq),
        in_specs=[
            pl.BlockSpec((tq, w), lambda b, g, i: (b * nq + i, g)),
            pl.BlockSpec((seq, w), lambda b, g, i: (b, groups + g)),
            pl.BlockSpec((seq, w), lambda b, g, i: (b, 2 * groups + g)),
        ],
        out_specs=pl.BlockSpec((tq, w), lambda b, g, i: (b * nq + i, g)),
        out_shape=jax.ShapeDtypeStruct((t, groups * w), BF16),
        scratch_shapes=[pltpu.VMEM((hp, 2 * tq, LANES), F32)],
        compiler_params=_params("parallel", "parallel", "arbitrary"),
        name="sb_attention",
    )(qkv, qkv, qkv)


def _mlstm_body(qk_ref, v_ref, op_ref, gcol_ref, grow_ref, cw_ref, bcol_ref, brow_ref, hg_ref,
                o_ref, xs_ref, st_ref, m_ref, *, chunk):
    L = chunk
    c = pl.program_id(1)
    dqk = ML_HEADS * ML_QK_DIM

    @pl.when(c == 0)
    def _():
        xs_ref[0:8, :] = jnp.zeros((8, 2 * dqk), F32)
        st_ref[...] = jnp.zeros_like(st_ref)
        m_ref[...] = jnp.zeros_like(m_ref)

    xs_ref[8:L + 8, :] = qk_ref[...].astype(F32)
    y = cw_ref[0:1, :] * xs_ref[5:L + 5, :]
    for tap in range(1, ML_CONV):
        y = y + cw_ref[tap:tap + 1, :] * xs_ref[5 + tap:L + 5 + tap, :]
    xs_ref[0:8, :] = xs_ref[L:L + 8, :]
    y = y * jax.nn.sigmoid(y)
    q_all = (y[:, :dqk] * (ML_QK_DIM ** -0.5)).astype(BF16)
    k_all = y[:, dqk:].astype(BF16)

    gcol = gcol_ref[...] + bcol_ref[...]
    ti = lax.broadcasted_iota(jnp.int32, (L, L), 0)
    si = lax.broadcasted_iota(jnp.int32, (L, L), 1)
    causal = ti >= si
    tri = causal.astype(BF16)
    bcol = _split_dot_left(tri, _log_sigmoid(gcol))
    grow = grow_ref[0] + brow_ref[...]
    brow = _split_dot(_log_sigmoid(grow), (si >= ti).astype(BF16))

    lane = lax.broadcasted_iota(jnp.int32, (1, LANES), 1)
    srow = lax.broadcasted_iota(jnp.int32, (LANES, 1), 0)
    ones_v = jnp.ones((L, ML_V_DIM), F32)

    for p in range(ML_HEADS // 2):
        qp = q_all[:, p * LANES:(p + 1) * LANES]
        kp = k_all[:, p * LANES:(p + 1) * LANES]
        state = st_ref[p]
        state_b = state.astype(BF16)
        upd = []
        dec = []
        for hh in range(2):
            h = 2 * p + hh
            sel = (lane >= ML_QK_DIM) if hh else (lane < ML_QK_DIM)
            qm = jnp.where(sel, qp, jnp.zeros_like(qp))
            km = jnp.where(sel, kp, jnp.zeros_like(kp))
            vh = v_ref[:, h * ML_V_DIM:(h + 1) * ML_V_DIM]
            b_c = bcol[:, ML_HEADS + h:ML_HEADS + h + 1]
            li_c = gcol[:, h:h + 1]
            b_r = brow[ML_HEADS + h:ML_HEADS + h + 1, :]
            li_r = grow[h:h + 1, :]
            m_prev = m_ref[h:h + 1, 0:1]

            dmat = jnp.where(causal, b_c - b_r + li_r, -jnp.inf)
            inter = b_c + m_prev
            m_t = jnp.maximum(jnp.max(dmat, axis=-1, keepdims=True), inter)
            wt = jnp.exp(dmat - m_t)
            a = jnp.exp(inter - m_t)
            s = _dot_nt(qm, kp) * wt
            qc = _dot(qm, state_b)
            num = _dot(s.astype(BF16), vh) + a * qc[:, :ML_V_DIM]
            den = jnp.sum(s, axis=-1, keepdims=True) + a * qc[:, ML_V_DIM:]
            hout = num / jnp.maximum(jnp.abs(den), jnp.exp(-m_t))

            b_last = b_c[L - 1:L, :]
            g = b_last - b_c + li_c
            m_new = jnp.maximum(b_last + m_prev, jnp.max(g, axis=0, keepdims=True))
            dec.append(jnp.exp(b_last + m_prev - m_new))
            wk = jnp.exp(g - m_new)
            v_aug = jnp.concatenate([vh.astype(F32), ones_v], axis=1)
            upd.append(_dot_tn(km, (wk * v_aug).astype(BF16)))
            m_ref[h:h + 1, :] = jnp.broadcast_to(m_new, (1, LANES))

            ms = jnp.mean(hout * hout, axis=-1, keepdims=True)
            hn = hout * lax.rsqrt(ms + EPS) * hg_ref[:, h * ML_V_DIM:(h + 1) * ML_V_DIM]
            og = jax.nn.sigmoid(op_ref[:, h * ML_V_DIM:(h + 1) * ML_V_DIM].astype(F32))
            o_ref[:, h * ML_V_DIM:(h + 1) * ML_V_DIM] = (og * hn).astype(o_ref.dtype)

        decay = jnp.where(srow < ML_QK_DIM, dec[0], dec[1])
        st_ref[p] = decay * state + upd[0] + upd[1]


def _split_dot_left(m, x):
    hi = x.astype(BF16)
    lo = (x - hi.astype(F32)).astype(BF16)
    return _dot(m, hi) + _dot(m, lo)


def _mlstm(proj, gates, conv_w, b_i, b_f, head_gain, batch, seq, *, chunk=256):
    t = batch * seq
    d = ML_HEADS * ML_V_DIM
    nc = seq // chunk
    grow = gates[:, :2 * ML_HEADS].reshape(batch, seq, 2 * ML_HEADS).transpose(0, 2, 1)
    bias = jnp.concatenate([b_i, b_f]).astype(F32)
    bcol = jnp.zeros((1, LANES), F32).at[0, :2 * ML_HEADS].set(bias)
    brow = bias.reshape(2 * ML_HEADS, 1)
    return pl.pallas_call(
        functools.partial(_mlstm_body, chunk=chunk),
        grid=(batch, nc),
        in_specs=[
            pl.BlockSpec((chunk, d), lambda b, c: (b * nc + c, 0)),
            pl.BlockSpec((chunk, d), lambda b, c: (b * nc + c, 1)),
            pl.BlockSpec((chunk, d), lambda b, c: (b * nc + c, 2)),
            pl.BlockSpec((chunk, LANES), lambda b, c: (b * nc + c, 0)),
            pl.BlockSpec((1, 2 * ML_HEADS, chunk), lambda b, c: (b, 0, c)),
            pl.BlockSpec((ML_CONV, d), lambda b, c: (0, 0)),
            pl.BlockSpec((1, LANES), lambda b, c: (0, 0)),
            pl.BlockSpec((2 * ML_HEADS, 1), lambda b, c: (0, 0)),
            pl.BlockSpec((1, d), lambda b, c: (0, 0)),
        ],
        out_specs=pl.BlockSpec((chunk, d), lambda b, c: (b * nc + c, 0)),
        out_shape=jax.ShapeDtypeStruct((t, d), BF16),
        scratch_shapes=[
            pltpu.VMEM((chunk + 8, d), F32),
            pltpu.VMEM((ML_HEADS // 2, LANES, 2 * ML_V_DIM), F32),
            pltpu.VMEM((ML_HEADS, LANES), F32),
        ],
        compiler_params=_params("parallel", "arbitrary"),
        name="mlstm",
    )(proj, proj, proj, gates, grow, conv_w.astype(F32), bcol, brow, head_gain.reshape(1, d))


def _mla_up_body(c_ref, pos_ref, gq_ref, gkv_ref, wq1_ref, wq2_ref, wk_ref, wv_ref, inv_ref,
                 q_ref, k_ref, v_ref):
    qr, kvr = MLA_Q_RANK, MLA_KV_RANK
    cq = c_ref[:, 0:qr]
    ckv = c_ref[:, 512:512 + kvr]
    kr1 = c_ref[:, 768:896]
    kr2 = c_ref[:, 896:1024]
    cqn = (cq * lax.rsqrt(jnp.mean(cq * cq, axis=-1, keepdims=True) + EPS) * gq_ref[...]).astype(BF16)
    ckvn = (ckv * lax.rsqrt(jnp.mean(ckv * ckv, axis=-1, keepdims=True) + EPS) * gkv_ref[...]).astype(BF16)

    ang = pos_ref[...].astype(F32) * inv_ref[...]
    cosv = jnp.cos(ang)
    sinv = jnp.sin(ang)
    lane = lax.broadcasted_iota(jnp.int32, (1, LANES), 1)
    half = MLA_ROPE // 2
    cpat = jnp.where(lane < MLA_NOPE, 1.0, jnp.where(lane < MLA_NOPE + MLA_ROPE, cosv, 0.0))
    spat = jnp.where(lane < MLA_NOPE, 0.0,
                     jnp.where(lane < MLA_NOPE + half, -sinv,
                               jnp.where(lane < MLA_NOPE + MLA_ROPE, sinv, 0.0)))
    scale = (MLA_NOPE + MLA_ROPE) ** -0.5 * LOG2_E
    cpat_q = cpat * scale
    spat_q = spat * scale

    a1 = _dot(cqn, wq1_ref[...])
    a2 = _dot(cqn, wq2_ref[...])
    kn = _dot(ckvn, wk_ref[...])
    vv = _dot(ckvn, wv_ref[...])
    rk = kr1 * cpat + kr2 * spat
    ones = [jnp.where(lane >= MLA_V, 1.0, 0.0), jnp.where(lane < MLA_V, 1.0, 0.0)]
    for h in range(MLA_HEADS):
        sl = slice(h * LANES, (h + 1) * LANES)
        q_ref[:, sl] = (a1[:, sl] * cpat_q + a2[:, sl] * spat_q).astype(BF16)
        k_ref[:, sl] = (kn[:, sl] + rk).astype(BF16)
        v_ref[:, sl] = (vv[:, sl] + ones[h % 2]).astype(BF16)


def _mla_up(cproj, pos, g_q, g_kv, wq1, wq2, wk, wv, inv_pat, *, tm=512):
    t = cproj.shape[0]
    hw = MLA_HEADS * LANES
    full = lambda shape: pl.BlockSpec(shape, lambda i: (0, 0))
    return pl.pallas_call(
        _mla_up_body,
        grid=(t // tm,),
        in_specs=[
            pl.BlockSpec((tm, cproj.shape[1]), lambda i: (i, 0)),
            pl.BlockSpec((tm, 1), lambda i: (i, 0)),
            full((1, MLA_Q_RANK)), full((1, MLA_KV_RANK)),
            full(wq1.shape), full(wq2.shape), full(wk.shape), full(wv.shape),
            full((1, LANES)),
        ],
        out_specs=[
            pl.BlockSpec((tm, hw), lambda i: (i, 0)),
            pl.BlockSpec((tm, hw), lambda i: (i, 0)),
            pl.BlockSpec((tm, hw), lambda i: (i, 0)),
        ],
        out_shape=[
            jax.ShapeDtypeStruct((t, hw), BF16),
            jax.ShapeDtypeStruct((t, hw), BF16),
            jax.ShapeDtypeStruct((t, hw), BF16),
        ],
        compiler_params=_params("parallel"),
        name="mla_up",
    )(cproj, pos, g_q.reshape(1, -1), g_kv.reshape(1, -1), wq1, wq2, wk, wv, inv_pat)


def _mla_attn_body(q_ref, k_ref, v_ref, o_ref, acc_ref, *, tq):
    i = pl.program_id(2)
    tk = tq
    lane = lax.broadcasted_iota(jnp.int32, (1, LANES), 1)
    acc_ref[...] = jnp.zeros_like(acc_ref)
    qs = [q_ref[:, 0:LANES], q_ref[:, LANES:2 * LANES]]

    def block(j, ms, masked):
        start = pl.multiple_of(j * tk, tk)
        out = []
        for hh in range(2):
            kb = k_ref[pl.ds(start, tk), hh * LANES:(hh + 1) * LANES]
            vb = v_ref[pl.ds(start, tk), hh * LANES:(hh + 1) * LANES]
            s = _dot_nt(qs[hh], kb)
            if masked:
                r = lax.broadcasted_iota(jnp.int32, (tq, tk), 0)
                cc = lax.broadcasted_iota(jnp.int32, (tq, tk), 1)
                s = jnp.where(cc <= r, s, -jnp.inf)
            m_new = jnp.maximum(ms[hh], jnp.max(s, axis=-1, keepdims=True))
            alpha = jnp.exp2(ms[hh] - m_new)
            pr = jnp.exp2(s - m_new)
            acc_ref[hh] = alpha * acc_ref[hh] + _dot(pr.astype(BF16), vb)
            out.append(m_new)
        return tuple(out)

    neg = jnp.full((tq, 1), -jnp.inf, F32)
    ms = lax.fori_loop(0, i, lambda j, st: block(j, st, False), (neg, neg))
    block(i, ms, True)
    r0 = acc_ref[0]
    r1 = acc_ref[1]
    r0 = r0 / pltpu.roll(r0, MLA_V, axis=1)
    r1 = r1 / pltpu.roll(r1, MLA_V, axis=1)
    o_ref[...] = jnp.where(lane < MLA_V, r0, r1).astype(o_ref.dtype)


def _mla_attention(q, k, v, batch, seq, *, tq=512):
    t = batch * seq
    pairs = MLA_HEADS // 2
    nq = seq // tq
    return pl.pallas_call(
        functools.partial(_mla_attn_body, tq=tq),
        grid=(batch, pairs, nq),
        in_specs=[
            pl.BlockSpec((tq, 2 * LANES), lambda b, p, i: (b * nq + i, p)),
            pl.BlockSpec((seq, 2 * LANES), lambda b, p, i: (b, p)),
            pl.BlockSpec((seq, 2 * LANES), lambda b, p, i: (b, p)),
        ],
        out_specs=pl.BlockSpec((tq, LANES), lambda b, p, i: (b * nq + i, p)),
        out_shape=jax.ShapeDtypeStruct((t, pairs * LANES), BF16),
        scratch_shapes=[pltpu.VMEM((2, tq, LANES), F32)],
        compiler_params=_params("parallel", "parallel", "arbitrary"),
        name="mla_attention",
    )(q, k, v)


def _sb_mixer(h, g, w_in, w_out, batch, seq):
    d = h.shape[1]
    scale = SB_HEAD_DIM ** -0.5 * LOG2_E
    w = jnp.concatenate([w_in[:, :d] * scale, w_in[:, d:]], axis=1).astype(BF16)
    qkv = _norm_matmul(h, g, w, BF16, name="sb_in_proj")
    o = _sb_attention(qkv, batch, seq)
    return _matmul_residual(o, w_out.astype(BF16), h)


def _mlstm_mixer(h, g, w_in, conv_w, b_i, b_f, head_gain, w_out, batch, seq):
    d = h.shape[1]
    w_main = w_in[:, :3 * d].astype(BF16)
    w_gate = jnp.zeros((d, LANES), F32).at[:, :2 * ML_HEADS].set(w_in[:, 3 * d:]).astype(BF16)
    proj = _norm_matmul(h, g, w_main, BF16, name="ml_in_proj")
    gates = _norm_matmul(h, g, w_gate, F32, name="ml_gate_proj")
    o = _mlstm(proj, gates, conv_w, b_i, b_f, head_gain, batch, seq)
    return _matmul_residual(o, w_out.astype(BF16), h)


def _mla_weights(w_in, w_uq, w_ukv):
    d = w_in.shape[0]
    qr, kvr, r = MLA_Q_RANK, MLA_KV_RANK, MLA_ROPE
    half = r // 2
    w_kr = w_in[:, qr + kvr:]
    w_kr_swap = jnp.concatenate([w_kr[:, half:], w_kr[:, :half]], axis=1)
    place = lambda wr: jnp.zeros((d, LANES), F32).at[:, MLA_NOPE:MLA_NOPE + r].set(wr)
    w_c = jnp.concatenate([
        w_in[:, :qr], jnp.zeros((d, 512 - qr), F32),
        w_in[:, qr:qr + kvr], place(w_kr), place(w_kr_swap)], axis=1)

    wq = w_uq.reshape(qr, MLA_HEADS, MLA_NOPE + r)
    q_rope = wq[:, :, MLA_NOPE:]
    q_swap = jnp.concatenate([q_rope[:, :, half:], q_rope[:, :, :half]], axis=2)
    pad = jnp.zeros((qr, MLA_HEADS, LANES - MLA_NOPE - r), F32)
    wq1 = jnp.concatenate([wq, pad], axis=2).reshape(qr, MLA_HEADS * LANES)
    wq2 = jnp.concatenate([jnp.zeros((qr, MLA_HEADS, MLA_NOPE), F32), q_swap, pad], axis=2)
    wq2 = wq2.reshape(qr, MLA_HEADS * LANES)

    wkv = w_ukv.reshape(kvr, MLA_HEADS, MLA_NOPE + MLA_V)
    wk = jnp.concatenate([wkv[:, :, :MLA_NOPE], jnp.zeros((kvr, MLA_HEADS, LANES - MLA_NOPE), F32)], axis=2)
    wk = wk.reshape(kvr, MLA_HEADS * LANES)
    wvp = wkv[:, :, MLA_NOPE:].reshape(kvr, MLA_HEADS // 2, 2, MLA_V)
    zv = jnp.zeros((kvr, MLA_HEADS // 2, MLA_V), F32)
    wv = jnp.stack([jnp.concatenate([wvp[:, :, 0], zv], axis=2),
                    jnp.concatenate([zv, wvp[:, :, 1]], axis=2)], axis=2).reshape(kvr, MLA_HEADS * LANES)
    return w_c.astype(BF16), wq1.astype(BF16), wq2.astype(BF16), wk.astype(BF16), wv.astype(BF16)


def _mla_mixer(h, g, positions, w_in, g_q, w_uq, g_kv, w_ukv, w_out, batch, seq):
    w_c, wq1, wq2, wk, wv = _mla_weights(w_in, w_uq, w_ukv)
    cproj = _norm_matmul(h, g, w_c, F32, name="mla_in_proj")
    half = MLA_ROPE // 2
    inv = ROPE_THETA ** (-jnp.arange(half, dtype=F32) / half)
    inv_pat = jnp.zeros((1, LANES), F32).at[0, MLA_NOPE:MLA_NOPE + MLA_ROPE].set(jnp.concatenate([inv, inv]))
    pos = positions.reshape(batch * seq, 1)
    q, k, v = _mla_up(cproj, pos, g_q, g_kv, wq1, wq2, wk, wv, inv_pat)
    o = _mla_attention(q, k, v, batch, seq)
    return _matmul_residual(o, w_out.astype(BF16), h)


def kernel(x, positions, ln_ffn1, ffn1_wi, ffn1_wo, ln_mix, ln_ffn2, ffn2_wi, ffn2_wo, sb_w_in, sb_w_out, ml_w_in, ml_conv_w, ml_b_igate, ml_b_fgate, ml_head_gain, ml_w_out, mla_w_in, mla_g_q, mla_w_uq, mla_g_kv, mla_w_ukv, mla_w_out, ln_final):
    batch, seq, d = x.shape
    depth = ln_mix.shape[0]
    h = x.reshape(batch * seq, d)
    for i in range(depth):
        h = _ffn(h, ln_ffn1[i], ffn1_wi[i].astype(BF16), ffn1_wo[i].astype(BF16))
        j = i // N_MIXERS
        kind = i % N_MIXERS
        if kind == 0:
            h = _sb_mixer(h, ln_mix[i], sb_w_in[j], sb_w_out[j], batch, seq)
        elif kind == 1:
            h = _mlstm_mixer(h, ln_mix[i], ml_w_in[j], ml_conv_w[j], ml_b_igate[j], ml_b_fgate[j],
                             ml_head_gain[j], ml_w_out[j], batch, seq)
        else:
            h = _mla_mixer(h, ln_mix[i], positions, mla_w_in[j], mla_g_q[j], mla_w_uq[j], mla_g_kv[j],
                           mla_w_ukv[j], mla_w_out[j], batch, seq)
        g_final = ln_final if i == depth - 1 else None
        h = _ffn(h, ln_ffn2[i], ffn2_wi[i].astype(BF16), ffn2_wo[i].astype(BF16), g_final)
    return h.reshape(batch, seq, d)
```

```python
import functools

import jax
import jax.numpy as jnp
from jax import lax
from jax.experimental import pallas as pl
from jax.experimental.pallas import tpu as pltpu

F32 = jnp.float32
BF16 = jnp.bfloat16

EPS = 1e-6
LANES = 128
VMEM_LIMIT = 48 * 1024 * 1024

SB_HEADS = 16
SB_HEAD_DIM = 64
ML_HEADS = 8
ML_QK_DIM = 64
ML_V_DIM = 128
ML_CONV = 4
MLA_HEADS = 16
MLA_NOPE = 64
MLA_ROPE = 32
MLA_V = 64
MLA_Q_RANK = 384
MLA_KV_RANK = 256
ROPE_THETA = 10000.0
LOG2_E = 1.4426950408889634
N_MIXERS = 3

EXP2_UNDERFLOW = -152.0


def _dot(a, b):
    return jnp.dot(a, b, preferred_element_type=F32)


def _dot_nt(a, b):
    return lax.dot_general(a, b, (((1,), (1,)), ((), ())), preferred_element_type=F32)


def _dot_tn(a, b):
    return lax.dot_general(a, b, (((0,), (0,)), ((), ())), preferred_element_type=F32)


def _split_dot(x, m):
    hi = x.astype(BF16)
    lo = (x - hi.astype(F32)).astype(BF16)
    return _dot(hi, m) + _dot(lo, m)


def _log_sigmoid(x):
    return jnp.minimum(x, 0.0) - jnp.log(1.0 + jnp.exp(-jnp.abs(x)))


def _params(*sem):
    return pltpu.CompilerParams(dimension_semantics=sem, vmem_limit_bytes=VMEM_LIMIT)


def _ffn_body(*refs, tf, fused_proj, final):
    refs = list(refs)
    h_ref = refs.pop(0)
    if fused_proj:
        a_ref, wp_ref = refs.pop(0), refs.pop(0)
    g_ref, wi_ref, wo_ref = refs.pop(0), refs.pop(0), refs.pop(0)
    if final:
        gf_ref = refs.pop(0)
    (o_ref,) = refs
    dff = wo_ref.shape[0]
    nf = dff // tf
    x = h_ref[...]
    if fused_proj:
        x = x + _dot(a_ref[...], wp_ref[...])
    ms = jnp.mean(x * x, axis=-1, keepdims=True)
    xn = (x * lax.rsqrt(ms + EPS) * g_ref[...]).astype(BF16)

    def gate_up(f):
        return (_dot(xn, wi_ref[:, f * tf:(f + 1) * tf]),
                _dot(xn, wi_ref[:, dff + f * tf:dff + (f + 1) * tf]))

    acc = None
    nxt = gate_up(0)
    for f in range(nf):
        gate, up = nxt
        if f + 1 < nf:
            nxt = gate_up(f + 1)
        act = (gate * jax.nn.sigmoid(gate) * up).astype(BF16)
        part = _dot(act, wo_ref[f * tf:(f + 1) * tf, :])
        acc = part if acc is None else acc + part
    y = x + 0.5 * acc
    if final:
        ms = jnp.mean(y * y, axis=-1, keepdims=True)
        y = y * lax.rsqrt(ms + EPS) * gf_ref[...]
    o_ref[...] = y


def _resident(shape):
    return pl.BlockSpec(shape, lambda i: (0,) * len(shape), pipeline_mode=pl.Buffered(1))


def _resident_layer(stacked, layer):
    return pl.BlockSpec((None,) + stacked.shape[1:], lambda i: (layer, 0, 0), pipeline_mode=pl.Buffered(1))


def _ffn(h, g, wi_all, wo_all, layer, *, proj=None, g_final=None, tm=512, tf=256):
    t, d = h.shape
    final = g_final is not None
    rows = lambda width: pl.BlockSpec((tm, width), lambda i: (i, 0))
    in_specs = [rows(d)]
    args = [h]
    if proj is not None:
        a, wp = proj
        in_specs += [rows(a.shape[1]), _resident(wp.shape)]
        args += [a, wp]
    in_specs += [_resident((1, d)), _resident_layer(wi_all, layer), _resident_layer(wo_all, layer)]
    args += [g.reshape(1, d), wi_all, wo_all]
    if final:
        in_specs.append(_resident((1, d)))
        args.append(g_final.reshape(1, d))
    return pl.pallas_call(
        functools.partial(_ffn_body, tf=tf, fused_proj=proj is not None, final=final),
        grid=(t // tm,),
        in_specs=in_specs,
        out_specs=rows(d),
        out_shape=jax.ShapeDtypeStruct((t, d), F32),
        compiler_params=_params("parallel"),
        name="ffn_final" if final else ("ffn_proj" if proj is not None else "ffn"),
    )(*args)


def _norm_matmul_body(x_ref, g_ref, *refs, sub):
    nw = len(refs) // 2
    w_refs, o_refs = refs[:nw], refs[nw:]
    for r in range(x_ref.shape[0] // sub):
        rows = slice(r * sub, (r + 1) * sub)
        x = x_ref[rows, :]
        ms = jnp.mean(x * x, axis=-1, keepdims=True)
        xn = (x * lax.rsqrt(ms + EPS) * g_ref[...]).astype(BF16)
        for w_ref, o_ref in zip(w_refs, o_refs):
            o_ref[rows, :] = _dot(xn, w_ref[...]).astype(o_ref.dtype)


def _norm_matmul(x, g, weights, out_dtypes, *, tm=1024, sub=512, name="norm_matmul"):
    t, k = x.shape
    rows = lambda width: pl.BlockSpec((tm, width), lambda i: (i, 0))
    return pl.pallas_call(
        functools.partial(_norm_matmul_body, sub=sub),
        grid=(t // tm,),
        in_specs=[rows(k), _resident((1, k))] + [_resident(w.shape) for w in weights],
        out_specs=[rows(w.shape[1]) for w in weights],
        out_shape=[jax.ShapeDtypeStruct((t, w.shape[1]), dt) for w, dt in zip(weights, out_dtypes)],
        compiler_params=_params("parallel"),
        name=name,
    )(x, g.reshape(1, k), *weights)


def _sb_attn_body(q_ref, k_ref, v_ref, o_ref, acc_ref, *, tq, hp):
    i = pl.program_id(2)
    tk = tq
    nh = 2 * hp
    lane = lax.broadcasted_iota(jnp.int32, (1, LANES), 1)
    qs = []
    for p in range(hp):
        qp = q_ref[:, p * LANES:(p + 1) * LANES]
        qs.append(jnp.concatenate([jnp.where(lane < SB_HEAD_DIM, qp, jnp.zeros_like(qp)),
                                   jnp.where(lane >= SB_HEAD_DIM, qp, jnp.zeros_like(qp))], axis=0))
    later = (lax.broadcasted_iota(jnp.int32, (tk, tk), 0)
             > lax.broadcasted_iota(jnp.int32, (tk, tk), 1)).astype(BF16)
    later2 = jnp.concatenate([later, later], axis=0)

    def step(j, c, diagonal):
        start = pl.multiple_of(j * tk, tk)
        z = jnp.concatenate(
            [_dot_nt(qs[p], k_ref[pl.ds(start, tk), p * LANES:(p + 1) * LANES]) for p in range(hp)],
            axis=0)
        nz = jnp.minimum(z, 0.0)
        pz = z - nz
        soft = jnp.log2(1.0 + jnp.exp2(nz - pz))
        log_beta = nz - soft
        log_fail = log_beta - z
        if diagonal:
            qrow = lax.broadcasted_iota(jnp.int32, (nh * tq, tk), 0) & (tq - 1)
            strict = lax.broadcasted_iota(jnp.int32, (nh * tq, tk), 1) < qrow
            log_fail = jnp.where(strict, log_fail, 0.0)
            log_beta = jnp.where(strict, log_beta, -jnp.inf)
        hi = log_fail.astype(BF16)
        lo = (log_fail - hi.astype(F32)).astype(BF16)
        between = _dot(jnp.concatenate([hi, lo], axis=1), later2) + c
        w = jnp.exp2(log_beta + between).astype(BF16)
        for p in range(hp):
            pv = _dot(w[2 * p * tq:2 * (p + 1) * tq], v_ref[pl.ds(start, tk), p * LANES:(p + 1) * LANES])
            if diagonal:
                acc_ref[p] = pv
            else:
                acc_ref[p] += pv
        c = c + jnp.sum(log_fail, axis=-1, keepdims=True)
        return c, jnp.max(c)

    c, cmax = step(i, jnp.zeros((nh * tq, 1), F32), True)

    def cond(carry):
        return jnp.logical_and(carry[0] >= 0, carry[1] > EXP2_UNDERFLOW)

    def body(carry):
        c, cmax = step(carry[0], carry[2], False)
        return carry[0] - 1, cmax, c

    lax.while_loop(cond, body, (i - 1, cmax, c))
    for p in range(hp):
        o_ref[:, p * LANES:(p + 1) * LANES] = jnp.where(
            lane < SB_HEAD_DIM, acc_ref[p, 0:tq], acc_ref[p, tq:2 * tq]).astype(o_ref.dtype)


def _sb_attention(qkv, batch, seq, *, tq=256, hp=2):
    t = batch * seq
    groups = SB_HEADS // 2 // hp
    nq = seq // tq
    w = hp * LANES
    return pl.pallas_call(
        functools.partial(_sb_attn_body, tq=tq, hp=hp),
        grid=(batch, groups, nq),
        in_specs=[
            pl.BlockSpec((tq, w), lambda b, g, i: (b * nq + i, g)),
            pl.BlockSpec((seq, w), lambda b, g, i: (b, groups + g)),
            pl.BlockSpec((seq, w), lambda b, g, i: (b, 2 * groups + g)),
        ],
        out_specs=pl.BlockSpec((tq, w), lambda b, g, i: (b * nq + i, g)),
        out_shape=jax.ShapeDtypeStruct((t, groups * w), BF16),
        scratch_shapes=[pltpu.VMEM((hp, 2 * tq, LANES), F32)],
        compiler_params=_params("parallel", "parallel", "arbitrary"),
        name="sb_attention",
    )(qkv, qkv, qkv)


def _mlstm_body(qk_ref, v_ref, op_ref, gcol_ref, grow_ref, cw_ref, bcol_ref, brow_ref, hg_ref,
                o_ref, xs_ref, st_ref, m_ref, *, chunk):
    L = chunk
    c = pl.program_id(1)
    dqk = ML_HEADS * ML_QK_DIM

    @pl.when(c == 0)
    def _():
        xs_ref[0:8, :] = jnp.zeros((8, 2 * dqk), F32)
        st_ref[...] = jnp.zeros_like(st_ref)
        m_ref[...] = jnp.zeros_like(m_ref)

    xs_ref[8:L + 8, :] = qk_ref[...].astype(F32)
    y = cw_ref[0:1, :] * xs_ref[5:L + 5, :]
    for tap in range(1, ML_CONV):
        y = y + cw_ref[tap:tap + 1, :] * xs_ref[5 + tap:L + 5 + tap, :]
    xs_ref[0:8, :] = xs_ref[L:L + 8, :]
    y = y * jax.nn.sigmoid(y)
    q_all = (y[:, :dqk] * (ML_QK_DIM ** -0.5)).astype(BF16)
    k_all = y[:, dqk:].astype(BF16)

    gcol = gcol_ref[...] + bcol_ref[...]
    ti = lax.broadcasted_iota(jnp.int32, (L, L), 0)
    si = lax.broadcasted_iota(jnp.int32, (L, L), 1)
    causal = ti >= si
    tri = causal.astype(BF16)
    bcol = _split_dot_left(tri, _log_sigmoid(gcol))
    grow = grow_ref[0] + brow_ref[...]
    brow = _split_dot(_log_sigmoid(grow), (si >= ti).astype(BF16))

    lane = lax.broadcasted_iota(jnp.int32, (1, LANES), 1)
    srow = lax.broadcasted_iota(jnp.int32, (LANES, 1), 0)
    ones_v = jnp.ones((L, ML_V_DIM), F32)

    for p in range(ML_HEADS // 2):
        qp = q_all[:, p * LANES:(p + 1) * LANES]
        kp = k_all[:, p * LANES:(p + 1) * LANES]
        state = st_ref[p]
        state_b = state.astype(BF16)
        upd = []
        dec = []
        for hh in range(2):
            h = 2 * p + hh
            sel = (lane >= ML_QK_DIM) if hh else (lane < ML_QK_DIM)
            qm = jnp.where(sel, qp, jnp.zeros_like(qp))
            km = jnp.where(sel, kp, jnp.zeros_like(kp))
            vh = v_ref[:, h * ML_V_DIM:(h + 1) * ML_V_DIM]
            b_c = bcol[:, ML_HEADS + h:ML_HEADS + h + 1]
            li_c = gcol[:, h:h + 1]
            b_r = brow[ML_HEADS + h:ML_HEADS + h + 1, :]
            li_r = grow[h:h + 1, :]
            m_prev = m_ref[h:h + 1, 0:1]

            dmat = jnp.where(causal, b_c - b_r + li_r, -jnp.inf)
            inter = b_c + m_prev
            m_t = jnp.maximum(jnp.max(dmat, axis=-1, keepdims=True), inter)
            wt = jnp.exp(dmat - m_t)
            a = jnp.exp(inter - m_t)
            s = _dot_nt(qm, kp) * wt
            qc = _dot(qm, state_b)
            num = _dot(s.astype(BF16), vh) + a * qc[:, :ML_V_DIM]
            den = jnp.sum(s, axis=-1, keepdims=True) + a * qc[:, ML_V_DIM:]
            hout = num / jnp.maximum(jnp.abs(den), jnp.exp(-m_t))

            b_last = b_c[L - 1:L, :]
            g = b_last - b_c + li_c
            m_new = jnp.maximum(b_last + m_prev, jnp.max(g, axis=0, keepdims=True))
            dec.append(jnp.exp(b_last + m_prev - m_new))
            wk = jnp.exp(g - m_new)
            v_aug = jnp.concatenate([vh.astype(F32), ones_v], axis=1)
            upd.append(_dot_tn(km, (wk * v_aug).astype(BF16)))
            m_ref[h:h + 1, :] = jnp.broadcast_to(m_new, (1, LANES))

            ms = jnp.mean(hout * hout, axis=-1, keepdims=True)
            hn = hout * lax.rsqrt(ms + EPS) * hg_ref[:, h * ML_V_DIM:(h + 1) * ML_V_DIM]
            og = jax.nn.sigmoid(op_ref[:, h * ML_V_DIM:(h + 1) * ML_V_DIM].astype(F32))
            o_ref[:, h * ML_V_DIM:(h + 1) * ML_V_DIM] = (og * hn).astype(o_ref.dtype)

        decay = jnp.where(srow < ML_QK_DIM, dec[0], dec[1])
        st_ref[p] = decay * state + upd[0] + upd[1]


def _split_dot_left(m, x):
    hi = x.astype(BF16)
    lo = (x - hi.astype(F32)).astype(BF16)
    return _dot(m, hi) + _dot(m, lo)


def _mlstm(proj, gates, conv_w, b_i, b_f, head_gain, batch, seq, *, chunk=256):
    t = batch * seq
    d = ML_HEADS * ML_V_DIM
    nc = seq // chunk
    grow = gates[:, :2 * ML_HEADS].reshape(batch, seq, 2 * ML_HEADS).transpose(0, 2, 1)
    bias = jnp.concatenate([b_i, b_f]).astype(F32)
    bcol = jnp.zeros((1, LANES), F32).at[0, :2 * ML_HEADS].set(bias)
    brow = bias.reshape(2 * ML_HEADS, 1)
    return pl.pallas_call(
        functools.partial(_mlstm_body, chunk=chunk),
        grid=(batch, nc),
        in_specs=[
            pl.BlockSpec((chunk, d), lambda b, c: (b * nc + c, 0)),
            pl.BlockSpec((chunk, d), lambda b, c: (b * nc + c, 1)),
            pl.BlockSpec((chunk, d), lambda b, c: (b * nc + c, 2)),
            pl.BlockSpec((chunk, LANES), lambda b, c: (b * nc + c, 0)),
            pl.BlockSpec((1, 2 * ML_HEADS, chunk), lambda b, c: (b, 0, c)),
            pl.BlockSpec((ML_CONV, d), lambda b, c: (0, 0)),
            pl.BlockSpec((1, LANES), lambda b, c: (0, 0)),
            pl.BlockSpec((2 * ML_HEADS, 1), lambda b, c: (0, 0)),
            pl.BlockSpec((1, d), lambda b, c: (0, 0)),
        ],
        out_specs=pl.BlockSpec((chunk, d), lambda b, c: (b * nc + c, 0)),
        out_shape=jax.ShapeDtypeStruct((t, d), BF16),
        scratch_shapes=[
            pltpu.VMEM((chunk + 8, d), F32),
            pltpu.VMEM((ML_HEADS // 2, LANES, 2 * ML_V_DIM), F32),
            pltpu.VMEM((ML_HEADS, LANES), F32),
        ],
        compiler_params=_params("parallel", "arbitrary"),
        name="mlstm",
    )(proj, proj, proj, gates, grow, conv_w.astype(F32), bcol, brow, head_gain.reshape(1, d))


def _mla_up_body(c_ref, pos_ref, gq_ref, gkv_ref, wq1_ref, wq2_ref, wk_ref, wv_ref, inv_ref,
                 q_ref, k_ref, v_ref):
    qr, kvr = MLA_Q_RANK, MLA_KV_RANK
    cq = c_ref[:, 0:qr]
    ckv = c_ref[:, 512:512 + kvr]
    kr1 = c_ref[:, 768:896]
    kr2 = c_ref[:, 896:1024]
    cqn = (cq * lax.rsqrt(jnp.mean(cq * cq, axis=-1, keepdims=True) + EPS) * gq_ref[...]).astype(BF16)
    ckvn = (ckv * lax.rsqrt(jnp.mean(ckv * ckv, axis=-1, keepdims=True) + EPS) * gkv_ref[...]).astype(BF16)

    ang = pos_ref[...].astype(F32) * inv_ref[...]
    cosv = jnp.cos(ang)
    sinv = jnp.sin(ang)
    lane = lax.broadcasted_iota(jnp.int32, (1, LANES), 1)
    half = MLA_ROPE // 2
    cpat = jnp.where(lane < MLA_NOPE, 1.0, jnp.where(lane < MLA_NOPE + MLA_ROPE, cosv, 0.0))
    spat = jnp.where(lane < MLA_NOPE, 0.0,
                     jnp.where(lane < MLA_NOPE + half, -sinv,
                               jnp.where(lane < MLA_NOPE + MLA_ROPE, sinv, 0.0)))
    scale = (MLA_NOPE + MLA_ROPE) ** -0.5 * LOG2_E
    cpat_q = cpat * scale
    spat_q = spat * scale

    a1 = _dot(cqn, wq1_ref[...])
    a2 = _dot(cqn, wq2_ref[...])
    kn = _dot(ckvn, wk_ref[...])
    vv = _dot(ckvn, wv_ref[...])
    rk = kr1 * cpat + kr2 * spat
    ones = [jnp.where(lane >= MLA_V, 1.0, 0.0), jnp.where(lane < MLA_V, 1.0, 0.0)]
    for h in range(MLA_HEADS):
        sl = slice(h * LANES, (h + 1) * LANES)
        q_ref[:, sl] = (a1[:, sl] * cpat_q + a2[:, sl] * spat_q).astype(BF16)
        k_ref[:, sl] = (kn[:, sl] + rk).astype(BF16)
        v_ref[:, sl] = (vv[:, sl] + ones[h % 2]).astype(BF16)


def _mla_up(cproj, pos, g_q, g_kv, wq1, wq2, wk, wv, inv_pat, *, tm=512):
    t = cproj.shape[0]
    hw = MLA_HEADS * LANES
    full = lambda shape: pl.BlockSpec(shape, lambda i: (0, 0))
    return pl.pallas_call(
        _mla_up_body,
        grid=(t // tm,),
        in_specs=[
            pl.BlockSpec((tm, cproj.shape[1]), lambda i: (i, 0)),
            pl.BlockSpec((tm, 1), lambda i: (i, 0)),
            full((1, MLA_Q_RANK)), full((1, MLA_KV_RANK)),
            full(wq1.shape), full(wq2.shape), full(wk.shape), full(wv.shape),
            full((1, LANES)),
        ],
        out_specs=[
            pl.BlockSpec((tm, hw), lambda i: (i, 0)),
            pl.BlockSpec((tm, hw), lambda i: (i, 0)),
            pl.BlockSpec((tm, hw), lambda i: (i, 0)),
        ],
        out_shape=[
            jax.ShapeDtypeStruct((t, hw), BF16),
            jax.ShapeDtypeStruct((t, hw), BF16),
            jax.ShapeDtypeStruct((t, hw), BF16),
        ],
        compiler_params=_params("parallel"),
        name="mla_up",
    )(cproj, pos, g_q.reshape(1, -1), g_kv.reshape(1, -1), wq1, wq2, wk, wv, inv_pat)


def _mla_attn_body(q_ref, k_ref, v_ref, o_ref, acc_ref, *, tq, hp):
    i = pl.program_id(2)
    tk = tq
    nh = 2 * hp
    lane = lax.broadcasted_iota(jnp.int32, (1, LANES), 1)
    acc_ref[...] = jnp.zeros_like(acc_ref)
    qs = [q_ref[:, h * LANES:(h + 1) * LANES] for h in range(nh)]

    def block(j, ms, masked):
        start = pl.multiple_of(j * tk, tk)
        out = []
        ss = [_dot_nt(qs[h], k_ref[pl.ds(start, tk), h * LANES:(h + 1) * LANES]) for h in range(nh)]
        for h in range(nh):
            vb = v_ref[pl.ds(start, tk), h * LANES:(h + 1) * LANES]
            s = ss[h]
            if masked:
                r = lax.broadcasted_iota(jnp.int32, (tq, tk), 0)
                cc = lax.broadcasted_iota(jnp.int32, (tq, tk), 1)
                s = jnp.where(cc <= r, s, -jnp.inf)
            m_new = jnp.maximum(ms[h], jnp.max(s, axis=-1, keepdims=True))
            alpha = jnp.exp2(ms[h] - m_new)
            pr = jnp.exp2(s - m_new)
            acc_ref[h] = alpha * acc_ref[h] + _dot(pr.astype(BF16), vb)
            out.append(m_new)
        return tuple(out)

    neg = jnp.full((tq, 1), -jnp.inf, F32)
    ms = lax.fori_loop(0, i, lambda j, st: block(j, st, False), (neg,) * nh)
    block(i, ms, True)
    for p in range(hp):
        r0 = acc_ref[2 * p]
        r1 = acc_ref[2 * p + 1]
        r0 = r0 / pltpu.roll(r0, MLA_V, axis=1)
        r1 = r1 / pltpu.roll(r1, MLA_V, axis=1)
        o_ref[:, p * LANES:(p + 1) * LANES] = jnp.where(lane < MLA_V, r0, r1).astype(o_ref.dtype)


def _mla_attention(q, k, v, batch, seq, *, tq=512, hp=2):
    t = batch * seq
    groups = MLA_HEADS // 2 // hp
    nq = seq // tq
    w = 2 * hp * LANES
    return pl.pallas_call(
        functools.partial(_mla_attn_body, tq=tq, hp=hp),
        grid=(batch, groups, nq),
        in_specs=[
            pl.BlockSpec((tq, w), lambda b, g, i: (b * nq + i, g)),
            pl.BlockSpec((seq, w), lambda b, g, i: (b, g)),
            pl.BlockSpec((seq, w), lambda b, g, i: (b, g)),
        ],
        out_specs=pl.BlockSpec((tq, hp * LANES), lambda b, g, i: (b * nq + i, g)),
        out_shape=jax.ShapeDtypeStruct((t, groups * hp * LANES), BF16),
        scratch_shapes=[pltpu.VMEM((2 * hp, tq, LANES), F32)],
        compiler_params=_params("parallel", "parallel", "arbitrary"),
        name="mla_attention",
    )(q, k, v)


def _sb_mixer(h, g, w_in, batch, seq):
    d = h.shape[1]
    scale = SB_HEAD_DIM ** -0.5 * LOG2_E
    w = jnp.concatenate([w_in[:, :d] * scale, w_in[:, d:]], axis=1).astype(BF16)
    (qkv,) = _norm_matmul(h, g, [w], [BF16], name="sb_in_proj")
    return _sb_attention(qkv, batch, seq)


def _mlstm_mixer(h, g, w_in, conv_w, b_i, b_f, head_gain, batch, seq):
    d = h.shape[1]
    w_main = w_in[:, :3 * d].astype(BF16)
    w_gate = jnp.zeros((d, LANES), F32).at[:, :2 * ML_HEADS].set(w_in[:, 3 * d:]).astype(BF16)
    proj, gates = _norm_matmul(h, g, [w_main, w_gate], [BF16, F32], name="ml_in_proj")
    return _mlstm(proj, gates, conv_w, b_i, b_f, head_gain, batch, seq)


def _mla_weights(w_in, w_uq, w_ukv):
    d = w_in.shape[0]
    qr, kvr, r = MLA_Q_RANK, MLA_KV_RANK, MLA_ROPE
    half = r // 2
    w_kr = w_in[:, qr + kvr:]
    w_kr_swap = jnp.concatenate([w_kr[:, half:], w_kr[:, :half]], axis=1)
    place = lambda wr: jnp.zeros((d, LANES), F32).at[:, MLA_NOPE:MLA_NOPE + r].set(wr)
    w_c = jnp.concatenate([
        w_in[:, :qr], jnp.zeros((d, 512 - qr), F32),
        w_in[:, qr:qr + kvr], place(w_kr), place(w_kr_swap)], axis=1)

    wq = w_uq.reshape(qr, MLA_HEADS, MLA_NOPE + r)
    q_rope = wq[:, :, MLA_NOPE:]
    q_swap = jnp.concatenate([q_rope[:, :, half:], q_rope[:, :, :half]], axis=2)
    pad = jnp.zeros((qr, MLA_HEADS, LANES - MLA_NOPE - r), F32)
    wq1 = jnp.concatenate([wq, pad], axis=2).reshape(qr, MLA_HEADS * LANES)
    wq2 = jnp.concatenate([jnp.zeros((qr, MLA_HEADS, MLA_NOPE), F32), q_swap, pad], axis=2)
    wq2 = wq2.reshape(qr, MLA_HEADS * LANES)

    wkv = w_ukv.reshape(kvr, MLA_HEADS, MLA_NOPE + MLA_V)
    wk = jnp.concatenate([wkv[:, :, :MLA_NOPE], jnp.zeros((kvr, MLA_HEADS, LANES - MLA_NOPE), F32)], axis=2)
    wk = wk.reshape(kvr, MLA_HEADS * LANES)
    wvp = wkv[:, :, MLA_NOPE:].reshape(kvr, MLA_HEADS // 2, 2, MLA_V)
    zv = jnp.zeros((kvr, MLA_HEADS // 2, MLA_V), F32)
    wv = jnp.stack([jnp.concatenate([wvp[:, :, 0], zv], axis=2),
                    jnp.concatenate([zv, wvp[:, :, 1]], axis=2)], axis=2).reshape(kvr, MLA_HEADS * LANES)
    return w_c.astype(BF16), wq1.astype(BF16), wq2.astype(BF16), wk.astype(BF16), wv.astype(BF16)


def _mla_mixer(h, g, positions, w_in, g_q, w_uq, g_kv, w_ukv, batch, seq):
    w_c, wq1, wq2, wk, wv = _mla_weights(w_in, w_uq, w_ukv)
    (cproj,) = _norm_matmul(h, g, [w_c], [F32], name="mla_in_proj")
    half = MLA_ROPE // 2
    inv = ROPE_THETA ** (-jnp.arange(half, dtype=F32) / half)
    inv_pat = jnp.zeros((1, LANES), F32).at[0, MLA_NOPE:MLA_NOPE + MLA_ROPE].set(jnp.concatenate([inv, inv]))
    pos = positions.reshape(batch * seq, 1)
    q, k, v = _mla_up(cproj, pos, g_q, g_kv, wq1, wq2, wk, wv, inv_pat)
    return _mla_attention(q, k, v, batch, seq)


def kernel(x, positions, ln_ffn1, ffn1_wi, ffn1_wo, ln_mix, ln_ffn2, ffn2_wi, ffn2_wo, sb_w_in, sb_w_out, ml_w_in, ml_conv_w, ml_b_igate, ml_b_fgate, ml_head_gain, ml_w_out, mla_w_in, mla_g_q, mla_w_uq, mla_g_kv, mla_w_ukv, mla_w_out, ln_final):
    batch, seq, d = x.shape
    depth = ln_mix.shape[0]
    h = x.reshape(batch * seq, d)
    wi1, wo1 = ffn1_wi.astype(BF16), ffn1_wo.astype(BF16)
    wi2, wo2 = ffn2_wi.astype(BF16), ffn2_wo.astype(BF16)
    for i in range(depth):
        h = _ffn(h, ln_ffn1[i], wi1, wo1, i)
        j = i // N_MIXERS
        kind = i % N_MIXERS
        if kind == 0:
            o = _sb_mixer(h, ln_mix[i], sb_w_in[j], batch, seq)
            w_out = sb_w_out[j]
        elif kind == 1:
            o = _mlstm_mixer(h, ln_mix[i], ml_w_in[j], ml_conv_w[j], ml_b_igate[j], ml_b_fgate[j],
                             ml_head_gain[j], batch, seq)
            w_out = ml_w_out[j]
        else:
            o = _mla_mixer(h, ln_mix[i], positions, mla_w_in[j], mla_g_q[j], mla_w_uq[j], mla_g_kv[j],
                           mla_w_ukv[j], batch, seq)
            w_out = mla_w_out[j]
        h = _ffn(h, ln_ffn2[i], wi2, wo2, i,
                 proj=(o, w_out.astype(BF16)), g_final=ln_final if i == depth - 1 else None)
    return h.reshape(batch, seq, d)
```

```python
import functools

import jax
import jax.numpy as jnp
from jax import lax
from jax.experimental import pallas as pl
from jax.experimental.pallas import tpu as pltpu

F32 = jnp.float32
BF16 = jnp.bfloat16

EPS = 1e-6
LANES = 128
VMEM_LIMIT = 48 * 1024 * 1024

SB_HEADS = 16
SB_HEAD_DIM = 64
ML_HEADS = 8
ML_QK_DIM = 64
ML_V_DIM = 128
ML_CONV = 4
MLA_HEADS = 16
MLA_NOPE = 64
MLA_ROPE = 32
MLA_V = 64
MLA_Q_RANK = 384
MLA_KV_RANK = 256
ROPE_THETA = 10000.0
LOG2_E = 1.4426950408889634
N_MIXERS = 3

EXP2_UNDERFLOW = -152.0


def _dot(a, b):
    return jnp.dot(a, b, preferred_element_type=F32)


def _dot_nt(a, b):
    return lax.dot_general(a, b, (((1,), (1,)), ((), ())), preferred_element_type=F32)


def _dot_tn(a, b):
    return lax.dot_general(a, b, (((0,), (0,)), ((), ())), preferred_element_type=F32)


def _split_dot(x, m):
    hi = x.astype(BF16)
    lo = (x - hi.astype(F32)).astype(BF16)
    return _dot(hi, m) + _dot(lo, m)


def _log_sigmoid(x):
    return jnp.minimum(x, 0.0) - jnp.log(1.0 + jnp.exp(-jnp.abs(x)))


def _params(*sem):
    return pltpu.CompilerParams(dimension_semantics=sem, vmem_limit_bytes=VMEM_LIMIT)


def _ffn_body(*refs, tf, fused_proj, final):
    refs = list(refs)
    h_ref = refs.pop(0)
    if fused_proj:
        a_ref, wp_ref = refs.pop(0), refs.pop(0)
    g_ref, wi_ref, wo_ref = refs.pop(0), refs.pop(0), refs.pop(0)
    if final:
        gf_ref = refs.pop(0)
    (o_ref,) = refs
    dff = wo_ref.shape[0]
    nf = dff // tf
    x = h_ref[...]
    if fused_proj:
        x = x + _dot(a_ref[...], wp_ref[...])
    ms = jnp.mean(x * x, axis=-1, keepdims=True)
    xn = (x * lax.rsqrt(ms + EPS) * g_ref[...]).astype(BF16)

    def gate_up(f):
        return (_dot(xn, wi_ref[:, f * tf:(f + 1) * tf]),
                _dot(xn, wi_ref[:, dff + f * tf:dff + (f + 1) * tf]))

    acc = None
    nxt = gate_up(0)
    for f in range(nf):
        gate, up = nxt
        if f + 1 < nf:
            nxt = gate_up(f + 1)
        act = (gate * jax.nn.sigmoid(gate) * up).astype(BF16)
        part = _dot(act, wo_ref[f * tf:(f + 1) * tf, :])
        acc = part if acc is None else acc + part
    y = x + 0.5 * acc
    if final:
        ms = jnp.mean(y * y, axis=-1, keepdims=True)
        y = y * lax.rsqrt(ms + EPS) * gf_ref[...]
    o_ref[...] = y


def _resident(shape):
    return pl.BlockSpec(shape, lambda i: (0,) * len(shape), pipeline_mode=pl.Buffered(1))


def _resident_layer(stacked, layer):
    return pl.BlockSpec((None,) + stacked.shape[1:], lambda i: (layer, 0, 0), pipeline_mode=pl.Buffered(1))


def _ffn(h, g, wi_all, wo_all, layer, *, proj=None, g_final=None, tm=512, tf=256):
    t, d = h.shape
    final = g_final is not None
    rows = lambda width: pl.BlockSpec((tm, width), lambda i: (i, 0))
    in_specs = [rows(d)]
    args = [h]
    if proj is not None:
        a, wp = proj
        in_specs += [rows(a.shape[1]), _resident(wp.shape)]
        args += [a, wp]
    in_specs += [_resident((1, d)), _resident_layer(wi_all, layer), _resident_layer(wo_all, layer)]
    args += [g.reshape(1, d), wi_all, wo_all]
    if final:
        in_specs.append(_resident((1, d)))
        args.append(g_final.reshape(1, d))
    return pl.pallas_call(
        functools.partial(_ffn_body, tf=tf, fused_proj=proj is not None, final=final),
        grid=(t // tm,),
        in_specs=in_specs,
        out_specs=rows(d),
        out_shape=jax.ShapeDtypeStruct((t, d), F32),
        compiler_params=_params("parallel"),
        name="ffn_final" if final else ("ffn_proj" if proj is not None else "ffn"),
    )(*args)


def _norm_matmul_body(x_ref, g_ref, *refs, sub):
    nw = len(refs) // 2
    w_refs, o_refs = refs[:nw], refs[nw:]
    for r in range(x_ref.shape[0] // sub):
        rows = slice(r * sub, (r + 1) * sub)
        x = x_ref[rows, :]
        ms = jnp.mean(x * x, axis=-1, keepdims=True)
        xn = (x * lax.rsqrt(ms + EPS) * g_ref[...]).astype(BF16)
        for w_ref, o_ref in zip(w_refs, o_refs):
            o_ref[rows, :] = _dot(xn, w_ref[...]).astype(o_ref.dtype)


def _norm_matmul(x, g, weights, out_dtypes, *, tm=1024, sub=512, name="norm_matmul"):
    t, k = x.shape
    rows = lambda width: pl.BlockSpec((tm, width), lambda i: (i, 0))
    return pl.pallas_call(
        functools.partial(_norm_matmul_body, sub=sub),
        grid=(t // tm,),
        in_specs=[rows(k), _resident((1, k))] + [_resident(w.shape) for w in weights],
        out_specs=[rows(w.shape[1]) for w in weights],
        out_shape=[jax.ShapeDtypeStruct((t, w.shape[1]), dt) for w, dt in zip(weights, out_dtypes)],
        compiler_params=_params("parallel"),
        name=name,
    )(x, g.reshape(1, k), *weights)


def _sb_attn_body(q_ref, k_ref, v_ref, o_ref, acc_ref, *, tq, hp):
    i = pl.program_id(2)
    tk = tq
    nh = 2 * hp
    lane = lax.broadcasted_iota(jnp.int32, (1, LANES), 1)
    qs = []
    for p in range(hp):
        qp = q_ref[:, p * LANES:(p + 1) * LANES]
        qs.append(jnp.concatenate([jnp.where(lane < SB_HEAD_DIM, qp, jnp.zeros_like(qp)),
                                   jnp.where(lane >= SB_HEAD_DIM, qp, jnp.zeros_like(qp))], axis=0))
    def weights(qst, start, width, c, rows_per_head, masked):
        z = jnp.concatenate(
            [_dot_nt(qst[p], k_ref[pl.ds(start, width), p * LANES:(p + 1) * LANES]) for p in range(hp)],
            axis=0)
        nz = jnp.minimum(z, 0.0)
        pz = z - nz
        soft = jnp.log2(1.0 + jnp.exp2(nz - pz))
        log_beta = nz - soft
        log_fail = log_beta - z
        if masked:
            qrow = lax.broadcasted_iota(jnp.int32, z.shape, 0) & (rows_per_head - 1)
            strict = lax.broadcasted_iota(jnp.int32, z.shape, 1) < qrow
            log_fail = jnp.where(strict, log_fail, 0.0)
            log_beta = jnp.where(strict, log_beta, -jnp.inf)
        later = (lax.broadcasted_iota(jnp.int32, (width, width), 0)
                 > lax.broadcasted_iota(jnp.int32, (width, width), 1)).astype(BF16)
        between = _dot(log_fail.astype(BF16), later) + c
        w = jnp.exp2(log_beta + between).astype(BF16)
        return w, jnp.sum(log_fail, axis=-1, keepdims=True)

    def values(start, width, p):
        return v_ref[pl.ds(start, width), p * LANES:(p + 1) * LANES]

    base = pl.multiple_of(i * tq, tq)
    w, c = weights(qs, base, tk, jnp.zeros((nh * tq, 1), F32), tq, True)
    for p in range(hp):
        acc_ref[p] = _dot(w[2 * p * tq:2 * (p + 1) * tq], values(base, tk, p))

    def cond(carry):
        return jnp.logical_and(carry[0] >= 0, carry[1] > EXP2_UNDERFLOW)

    def body(carry):
        j, _, c = carry
        start = pl.multiple_of(j * tk, tk)
        w, r = weights(qs, start, tk, c, tq, False)
        for p in range(hp):
            acc_ref[p] += _dot(w[2 * p * tq:2 * (p + 1) * tq], values(start, tk, p))
        c = c + r
        return j - 1, jnp.max(c), c

    lax.while_loop(cond, body, (i - 1, jnp.max(c), c))
    for p in range(hp):
        o_ref[:, p * LANES:(p + 1) * LANES] = jnp.where(
            lane < SB_HEAD_DIM, acc_ref[p, 0:tq], acc_ref[p, tq:2 * tq]).astype(o_ref.dtype)


def _sb_attention(qkv, batch, seq, *, tq=256, hp=4):
    t = batch * seq
    groups = SB_HEADS // 2 // hp
    nq = seq // tq
    w = hp * LANES
    return pl.pallas_call(
        functools.partial(_sb_attn_body, tq=tq, hp=hp),
        grid=(batch, groups, nq),
        in_specs=[
            pl.BlockSpec((tq, w), lambda b, g, i: (b * nq + i, g)),
            pl.BlockSpec((seq, w), lambda b, g, i: (b, groups + g)),
            pl.BlockSpec((seq, w), lambda b, g, i: (b, 2 * groups + g)),
        ],
        out_specs=pl.BlockSpec((tq, w), lambda b, g, i: (b * nq + i, g)),
        out_shape=jax.ShapeDtypeStruct((t, groups * w), BF16),
        scratch_shapes=[pltpu.VMEM((hp, 2 * tq, LANES), F32)],
        compiler_params=_params("parallel", "parallel", "arbitrary"),
        name="sb_attention",
    )(qkv, qkv, qkv)


def _mlstm_body(qk_ref, v_ref, op_ref, gcol_ref, grow_ref, cw_ref, bcol_ref, brow_ref, hg_ref,
                o_ref, xs_ref, st_ref, m_ref, *, chunk):
    L = chunk
    c = pl.program_id(1)
    dqk = ML_HEADS * ML_QK_DIM

    @pl.when(c == 0)
    def _():
        xs_ref[0:8, :] = jnp.zeros((8, 2 * dqk), F32)
        st_ref[...] = jnp.zeros_like(st_ref)
        m_ref[...] = jnp.zeros_like(m_ref)

    xs_ref[8:L + 8, :] = qk_ref[...].astype(F32)
    y = cw_ref[0:1, :] * xs_ref[5:L + 5, :]
    for tap in range(1, ML_CONV):
        y = y + cw_ref[tap:tap + 1, :] * xs_ref[5 + tap:L + 5 + tap, :]
    xs_ref[0:8, :] = xs_ref[L:L + 8, :]
    y = y * jax.nn.sigmoid(y)
    q_all = (y[:, :dqk] * (ML_QK_DIM ** -0.5)).astype(BF16)
    k_all = y[:, dqk:].astype(BF16)

    gcol = gcol_ref[...] + bcol_ref[...]
    ti = lax.broadcasted_iota(jnp.int32, (L, L), 0)
    si = lax.broadcasted_iota(jnp.int32, (L, L), 1)
    causal = ti >= si
    tri = causal.astype(BF16)
    bcol = _split_dot_left(tri, _log_sigmoid(gcol))
    grow = grow_ref[0] + brow_ref[...]
    brow = _split_dot(_log_sigmoid(grow), (si >= ti).astype(BF16))

    lane = lax.broadcasted_iota(jnp.int32, (1, LANES), 1)
    srow = lax.broadcasted_iota(jnp.int32, (LANES, 1), 0)
    ones_v = jnp.ones((L, ML_V_DIM), F32)

    for p in range(ML_HEADS // 2):
        qp = q_all[:, p * LANES:(p + 1) * LANES]
        kp = k_all[:, p * LANES:(p + 1) * LANES]
        state = st_ref[p]
        state_b = state.astype(BF16)
        upd = []
        dec = []
        for hh in range(2):
            h = 2 * p + hh
            sel = (lane >= ML_QK_DIM) if hh else (lane < ML_QK_DIM)
            qm = jnp.where(sel, qp, jnp.zeros_like(qp))
            km = jnp.where(sel, kp, jnp.zeros_like(kp))
            vh = v_ref[:, h * ML_V_DIM:(h + 1) * ML_V_DIM]
            b_c = bcol[:, ML_HEADS + h:ML_HEADS + h + 1]
            li_c = gcol[:, h:h + 1]
            b_r = brow[ML_HEADS + h:ML_HEADS + h + 1, :]
            li_r = grow[h:h + 1, :]
            m_prev = m_ref[h:h + 1, 0:1]

            dmat = jnp.where(causal, b_c - b_r + li_r, -jnp.inf)
            inter = b_c + m_prev
            m_t = jnp.maximum(jnp.max(dmat, axis=-1, keepdims=True), inter)
            wt = jnp.exp(dmat - m_t)
            a = jnp.exp(inter - m_t)
            s = _dot_nt(qm, kp) * wt
            qc = _dot(qm, state_b)
            num = _dot(s.astype(BF16), vh) + a * qc[:, :ML_V_DIM]
            den = jnp.sum(s, axis=-1, keepdims=True) + a * qc[:, ML_V_DIM:]
            hout = num / jnp.maximum(jnp.abs(den), jnp.exp(-m_t))

            b_last = b_c[L - 1:L, :]
            g = b_last - b_c + li_c
            m_new = jnp.maximum(b_last + m_prev, jnp.max(g, axis=0, keepdims=True))
            dec.append(jnp.exp(b_last + m_prev - m_new))
            wk = jnp.exp(g - m_new)
            v_aug = jnp.concatenate([vh.astype(F32), ones_v], axis=1)
            upd.append(_dot_tn(km, (wk * v_aug).astype(BF16)))
            m_ref[h:h + 1, :] = jnp.broadcast_to(m_new, (1, LANES))

            ms = jnp.mean(hout * hout, axis=-1, keepdims=True)
            hn = hout * lax.rsqrt(ms + EPS) * hg_ref[:, h * ML_V_DIM:(h + 1) * ML_V_DIM]
            og = jax.nn.sigmoid(op_ref[:, h * ML_V_DIM:(h + 1) * ML_V_DIM].astype(F32))
            o_ref[:, h * ML_V_DIM:(h + 1) * ML_V_DIM] = (og * hn).astype(o_ref.dtype)

        decay = jnp.where(srow < ML_QK_DIM, dec[0], dec[1])
        st_ref[p] = decay * state + upd[0] + upd[1]


def _split_dot_left(m, x):
    hi = x.astype(BF16)
    lo = (x - hi.astype(F32)).astype(BF16)
    return _dot(m, hi) + _dot(m, lo)


def _mlstm(proj, gates, conv_w, b_i, b_f, head_gain, batch, seq, *, chunk=256):
    t = batch * seq
    d = ML_HEADS * ML_V_DIM
    nc = seq // chunk
    grow = gates[:, :2 * ML_HEADS].reshape(batch, seq, 2 * ML_HEADS).transpose(0, 2, 1)
    bias = jnp.concatenate([b_i, b_f]).astype(F32)
    bcol = jnp.zeros((1, LANES), F32).at[0, :2 * ML_HEADS].set(bias)
    brow = bias.reshape(2 * ML_HEADS, 1)
    return pl.pallas_call(
        functools.partial(_mlstm_body, chunk=chunk),
        grid=(batch, nc),
        in_specs=[
            pl.BlockSpec((chunk, d), lambda b, c: (b * nc + c, 0)),
            pl.BlockSpec((chunk, d), lambda b, c: (b * nc + c, 1)),
            pl.BlockSpec((chunk, d), lambda b, c: (b * nc + c, 2)),
            pl.BlockSpec((chunk, LANES), lambda b, c: (b * nc + c, 0)),
            pl.BlockSpec((1, 2 * ML_HEADS, chunk), lambda b, c: (b, 0, c)),
            pl.BlockSpec((ML_CONV, d), lambda b, c: (0, 0)),
            pl.BlockSpec((1, LANES), lambda b, c: (0, 0)),
            pl.BlockSpec((2 * ML_HEADS, 1), lambda b, c: (0, 0)),
            pl.BlockSpec((1, d), lambda b, c: (0, 0)),
        ],
        out_specs=pl.BlockSpec((chunk, d), lambda b, c: (b * nc + c, 0)),
        out_shape=jax.ShapeDtypeStruct((t, d), BF16),
        scratch_shapes=[
            pltpu.VMEM((chunk + 8, d), F32),
            pltpu.VMEM((ML_HEADS // 2, LANES, 2 * ML_V_DIM), F32),
            pltpu.VMEM((ML_HEADS, LANES), F32),
        ],
        compiler_params=_params("parallel", "arbitrary"),
        name="mlstm",
    )(proj, proj, proj, gates, grow, conv_w.astype(F32), bcol, brow, head_gain.reshape(1, d))


def _mla_up_body(c_ref, pos_ref, gq_ref, gkv_ref, wq1_ref, wq2_ref, wk_ref, wv_ref, inv_ref,
                 q_ref, k_ref, v_ref):
    qr, kvr = MLA_Q_RANK, MLA_KV_RANK
    cq = c_ref[:, 0:qr]
    ckv = c_ref[:, 512:512 + kvr]
    kr1 = c_ref[:, 768:896]
    kr2 = c_ref[:, 896:1024]
    cqn = (cq * lax.rsqrt(jnp.mean(cq * cq, axis=-1, keepdims=True) + EPS) * gq_ref[...]).astype(BF16)
    ckvn = (ckv * lax.rsqrt(jnp.mean(ckv * ckv, axis=-1, keepdims=True) + EPS) * gkv_ref[...]).astype(BF16)

    ang = pos_ref[...].astype(F32) * inv_ref[...]
    cosv = jnp.cos(ang)
    sinv = jnp.sin(ang)
    lane = lax.broadcasted_iota(jnp.int32, (1, LANES), 1)
    half = MLA_ROPE // 2
    cpat = jnp.where(lane < MLA_NOPE, 1.0, jnp.where(lane < MLA_NOPE + MLA_ROPE, cosv, 0.0))
    spat = jnp.where(lane < MLA_NOPE, 0.0,
                     jnp.where(lane < MLA_NOPE + half, -sinv,
                               jnp.where(lane < MLA_NOPE + MLA_ROPE, sinv, 0.0)))
    scale = (MLA_NOPE + MLA_ROPE) ** -0.5 * LOG2_E
    cpat_q = cpat * scale
    spat_q = spat * scale

    a1 = _dot(cqn, wq1_ref[...])
    a2 = _dot(cqn, wq2_ref[...])
    kn = _dot(ckvn, wk_ref[...])
    vv = _dot(ckvn, wv_ref[...])
    rk = kr1 * cpat + kr2 * spat
    ones = [jnp.where(lane >= MLA_V, 1.0, 0.0), jnp.where(lane < MLA_V, 1.0, 0.0)]
    for h in range(MLA_HEADS):
        sl = slice(h * LANES, (h + 1) * LANES)
        q_ref[:, sl] = (a1[:, sl] * cpat_q + a2[:, sl] * spat_q).astype(BF16)
        k_ref[:, sl] = (kn[:, sl] + rk).astype(BF16)
        v_ref[:, sl] = (vv[:, sl] + ones[h % 2]).astype(BF16)


def _mla_up(cproj, pos, g_q, g_kv, wq1, wq2, wk, wv, inv_pat, *, tm=512):
    t = cproj.shape[0]
    hw = MLA_HEADS * LANES
    full = lambda shape: pl.BlockSpec(shape, lambda i: (0, 0))
    return pl.pallas_call(
        _mla_up_body,
        grid=(t // tm,),
        in_specs=[
            pl.BlockSpec((tm, cproj.shape[1]), lambda i: (i, 0)),
            pl.BlockSpec((tm, 1), lambda i: (i, 0)),
            full((1, MLA_Q_RANK)), full((1, MLA_KV_RANK)),
            full(wq1.shape), full(wq2.shape), full(wk.shape), full(wv.shape),
            full((1, LANES)),
        ],
        out_specs=[
            pl.BlockSpec((tm, hw), lambda i: (i, 0)),
            pl.BlockSpec((tm, hw), lambda i: (i, 0)),
            pl.BlockSpec((tm, hw), lambda i: (i, 0)),
        ],
        out_shape=[
            jax.ShapeDtypeStruct((t, hw), BF16),
            jax.ShapeDtypeStruct((t, hw), BF16),
            jax.ShapeDtypeStruct((t, hw), BF16),
        ],
        compiler_params=_params("parallel"),
        name="mla_up",
    )(cproj, pos, g_q.reshape(1, -1), g_kv.reshape(1, -1), wq1, wq2, wk, wv, inv_pat)


def _mla_attn_body(q_ref, k_ref, v_ref, o_ref, acc_ref, *, tq, hp):
    i = pl.program_id(2)
    tk = tq
    nh = 2 * hp
    lane = lax.broadcasted_iota(jnp.int32, (1, LANES), 1)
    acc_ref[...] = jnp.zeros_like(acc_ref)
    qs = [q_ref[:, h * LANES:(h + 1) * LANES] for h in range(nh)]

    def block(j, ms, masked):
        start = pl.multiple_of(j * tk, tk)
        out = []
        score = lambda h: _dot_nt(qs[h], k_ref[pl.ds(start, tk), h * LANES:(h + 1) * LANES])
        ss = [score(h) for h in range(min(2, nh))]
        for h in range(nh):
            if h + 2 < nh:
                ss.append(score(h + 2))
            vb = v_ref[pl.ds(start, tk), h * LANES:(h + 1) * LANES]
            s = ss[h]
            if masked:
                r = lax.broadcasted_iota(jnp.int32, (tq, tk), 0)
                cc = lax.broadcasted_iota(jnp.int32, (tq, tk), 1)
                s = jnp.where(cc <= r, s, -jnp.inf)
            m_new = jnp.maximum(ms[h], jnp.max(s, axis=-1, keepdims=True))
            alpha = jnp.exp2(ms[h] - m_new)
            pr = jnp.exp2(s - m_new)
            acc_ref[h] = alpha * acc_ref[h] + _dot(pr.astype(BF16), vb)
            out.append(m_new)
        return tuple(out)

    neg = jnp.full((tq, 1), -jnp.inf, F32)
    ms = lax.fori_loop(0, i, lambda j, st: block(j, st, False), (neg,) * nh)
    block(i, ms, True)
    for p in range(hp):
        r0 = acc_ref[2 * p]
        r1 = acc_ref[2 * p + 1]
        r0 = r0 / pltpu.roll(r0, MLA_V, axis=1)
        r1 = r1 / pltpu.roll(r1, MLA_V, axis=1)
        o_ref[:, p * LANES:(p + 1) * LANES] = jnp.where(lane < MLA_V, r0, r1).astype(o_ref.dtype)


def _mla_attention(q, k, v, batch, seq, *, tq=512, hp=2):
    t = batch * seq
    groups = MLA_HEADS // 2 // hp
    nq = seq // tq
    w = 2 * hp * LANES
    return pl.pallas_call(
        functools.partial(_mla_attn_body, tq=tq, hp=hp),
        grid=(batch, groups, nq),
        in_specs=[
            pl.BlockSpec((tq, w), lambda b, g, i: (b * nq + i, g)),
            pl.BlockSpec((seq, w), lambda b, g, i: (b, g)),
            pl.BlockSpec((seq, w), lambda b, g, i: (b, g)),
        ],
        out_specs=pl.BlockSpec((tq, hp * LANES), lambda b, g, i: (b * nq + i, g)),
        out_shape=jax.ShapeDtypeStruct((t, groups * hp * LANES), BF16),
        scratch_shapes=[pltpu.VMEM((2 * hp, tq, LANES), F32)],
        compiler_params=_params("parallel", "parallel", "arbitrary"),
        name="mla_attention",
    )(q, k, v)


def _sb_mixer(h, g, w_in, batch, seq):
    d = h.shape[1]
    scale = SB_HEAD_DIM ** -0.5 * LOG2_E
    w = jnp.concatenate([w_in[:, :d] * scale, w_in[:, d:]], axis=1).astype(BF16)
    (qkv,) = _norm_matmul(h, g, [w], [BF16], name="sb_in_proj")
    return _sb_attention(qkv, batch, seq)


def _mlstm_mixer(h, g, w_in, conv_w, b_i, b_f, head_gain, batch, seq):
    d = h.shape[1]
    w_main = w_in[:, :3 * d].astype(BF16)
    w_gate = jnp.zeros((d, LANES), F32).at[:, :2 * ML_HEADS].set(w_in[:, 3 * d:]).astype(BF16)
    proj, gates = _norm_matmul(h, g, [w_main, w_gate], [BF16, F32], name="ml_in_proj")
    return _mlstm(proj, gates, conv_w, b_i, b_f, head_gain, batch, seq)


def _mla_weights(w_in, w_uq, w_ukv):
    d = w_in.shape[0]
    qr, kvr, r = MLA_Q_RANK, MLA_KV_RANK, MLA_ROPE
    half = r // 2
    w_kr = w_in[:, qr + kvr:]
    w_kr_swap = jnp.concatenate([w_kr[:, half:], w_kr[:, :half]], axis=1)
    place = lambda wr: jnp.zeros((d, LANES), F32).at[:, MLA_NOPE:MLA_NOPE + r].set(wr)
    w_c = jnp.concatenate([
        w_in[:, :qr], jnp.zeros((d, 512 - qr), F32),
        w_in[:, qr:qr + kvr], place(w_kr), place(w_kr_swap)], axis=1)

    wq = w_uq.reshape(qr, MLA_HEADS, MLA_NOPE + r)
    q_rope = wq[:, :, MLA_NOPE:]
    q_swap = jnp.concatenate([q_rope[:, :, half:], q_rope[:, :, :half]], axis=2)
    pad = jnp.zeros((qr, MLA_HEADS, LANES - MLA_NOPE - r), F32)
    wq1 = jnp.concatenate([wq, pad], axis=2).reshape(qr, MLA_HEADS * LANES)
    wq2 = jnp.concatenate([jnp.zeros((qr, MLA_HEADS, MLA_NOPE), F32), q_swap, pad], axis=2)
    wq2 = wq2.reshape(qr, MLA_HEADS * LANES)

    wkv = w_ukv.reshape(kvr, MLA_HEADS, MLA_NOPE + MLA_V)
    wk = jnp.concatenate([wkv[:, :, :MLA_NOPE], jnp.zeros((kvr, MLA_HEADS, LANES - MLA_NOPE), F32)], axis=2)
    wk = wk.reshape(kvr, MLA_HEADS * LANES)
    wvp = wkv[:, :, MLA_NOPE:].reshape(kvr, MLA_HEADS // 2, 2, MLA_V)
    zv = jnp.zeros((kvr, MLA_HEADS // 2, MLA_V), F32)
    wv = jnp.stack([jnp.concatenate([wvp[:, :, 0], zv], axis=2),
                    jnp.concatenate([zv, wvp[:, :, 1]], axis=2)], axis=2).reshape(kvr, MLA_HEADS * LANES)
    return w_c.astype(BF16), wq1.astype(BF16), wq2.astype(BF16), wk.astype(BF16), wv.astype(BF16)


def _mla_mixer(h, g, positions, w_in, g_q, w_uq, g_kv, w_ukv, batch, seq):
    w_c, wq1, wq2, wk, wv = _mla_weights(w_in, w_uq, w_ukv)
    (cproj,) = _norm_matmul(h, g, [w_c], [F32], name="mla_in_proj")
    half = MLA_ROPE // 2
    inv = ROPE_THETA ** (-jnp.arange(half, dtype=F32) / half)
    inv_pat = jnp.zeros((1, LANES), F32).at[0, MLA_NOPE:MLA_NOPE + MLA_ROPE].set(jnp.concatenate([inv, inv]))
    pos = positions.reshape(batch * seq, 1)
    q, k, v = _mla_up(cproj, pos, g_q, g_kv, wq1, wq2, wk, wv, inv_pat)
    return _mla_attention(q, k, v, batch, seq)


def kernel(x, positions, ln_ffn1, ffn1_wi, ffn1_wo, ln_mix, ln_ffn2, ffn2_wi, ffn2_wo, sb_w_in, sb_w_out, ml_w_in, ml_conv_w, ml_b_igate, ml_b_fgate, ml_head_gain, ml_w_out, mla_w_in, mla_g_q, mla_w_uq, mla_g_kv, mla_w_ukv, mla_w_out, ln_final):
    batch, seq, d = x.shape
    depth = ln_mix.shape[0]
    h = x.reshape(batch * seq, d)
    wi1, wo1 = ffn1_wi.astype(BF16), ffn1_wo.astype(BF16)
    wi2, wo2 = ffn2_wi.astype(BF16), ffn2_wo.astype(BF16)
    for i in range(depth):
        h = _ffn(h, ln_ffn1[i], wi1, wo1, i)
        j = i // N_MIXERS
        kind = i % N_MIXERS
        if kind == 0:
            o = _sb_mixer(h, ln_mix[i], sb_w_in[j], batch, seq)
            w_out = sb_w_out[j]
        elif kind == 1:
            o = _mlstm_mixer(h, ln_mix[i], ml_w_in[j], ml_conv_w[j], ml_b_igate[j], ml_b_fgate[j],
                             ml_head_gain[j], batch, seq)
            w_out = ml_w_out[j]
        else:
            o = _mla_mixer(h, ln_mix[i], positions, mla_w_in[j], mla_g_q[j], mla_w_uq[j], mla_g_kv[j],
                           mla_w_ukv[j], batch, seq)
            w_out = mla_w_out[j]
        h = _ffn(h, ln_ffn2[i], wi2, wo2, i,
                 proj=(o, w_out.astype(BF16)), g_final=ln_final if i == depth - 1 else None)
    return h.reshape(batch, seq, d)
```

```python
import functools

import jax
import jax.numpy as jnp
from jax import lax
from jax.experimental import pallas as pl
from jax.experimental.pallas import tpu as pltpu

F32 = jnp.float32
BF16 = jnp.bfloat16

EPS = 1e-6
LANES = 128
VMEM_LIMIT = 48 * 1024 * 1024

SB_HEADS = 16
SB_HEAD_DIM = 64
ML_HEADS = 8
ML_QK_DIM = 64
ML_V_DIM = 128
ML_CONV = 4
MLA_HEADS = 16
MLA_NOPE = 64
MLA_ROPE = 32
MLA_V = 64
MLA_Q_RANK = 384
MLA_KV_RANK = 256
ROPE_THETA = 10000.0
LOG2_E = 1.4426950408889634
N_MIXERS = 3

EXP2_UNDERFLOW = -152.0


def _dot(a, b):
    return jnp.dot(a, b, preferred_element_type=F32)


def _dot_nt(a, b):
    return lax.dot_general(a, b, (((1,), (1,)), ((), ())), preferred_element_type=F32)


def _dot_tn(a, b):
    return lax.dot_general(a, b, (((0,), (0,)), ((), ())), preferred_element_type=F32)


def _split_dot(x, m):
    hi = x.astype(BF16)
    lo = (x - hi.astype(F32)).astype(BF16)
    return _dot(hi, m) + _dot(lo, m)


def _log_sigmoid(x):
    return jnp.minimum(x, 0.0) - jnp.log(1.0 + jnp.exp(-jnp.abs(x)))


def _params(*sem):
    return pltpu.CompilerParams(dimension_semantics=sem, vmem_limit_bytes=VMEM_LIMIT)


def _ffn_body(*refs, tf, sub, fused_proj, final):
    refs = list(refs)
    h_ref = refs.pop(0)
    if fused_proj:
        a_ref, wp_ref = refs.pop(0), refs.pop(0)
    g_ref, wi_ref, wo_ref = refs.pop(0), refs.pop(0), refs.pop(0)
    if final:
        gf_ref = refs.pop(0)
    (o_ref,) = refs
    dff = wo_ref.shape[0]
    nf = dff // tf
    for r in range(h_ref.shape[0] // sub):
        rows = slice(r * sub, (r + 1) * sub)
        x = h_ref[rows, :]
        if fused_proj:
            x = x + _dot(a_ref[rows, :], wp_ref[...])
        ms = jnp.mean(x * x, axis=-1, keepdims=True)
        xn = (x * lax.rsqrt(ms + EPS) * g_ref[...]).astype(BF16)

        def gate_up(f):
            return (_dot(xn, wi_ref[:, f * tf:(f + 1) * tf]),
                    _dot(xn, wi_ref[:, dff + f * tf:dff + (f + 1) * tf]))

        acc = None
        nxt = gate_up(0)
        for f in range(nf):
            gate, up = nxt
            if f + 1 < nf:
                nxt = gate_up(f + 1)
            act = (gate * jax.nn.sigmoid(gate) * up).astype(BF16)
            part = _dot(act, wo_ref[f * tf:(f + 1) * tf, :])
            acc = part if acc is None else acc + part
        y = x + 0.5 * acc
        if final:
            ms = jnp.mean(y * y, axis=-1, keepdims=True)
            y = y * lax.rsqrt(ms + EPS) * gf_ref[...]
        o_ref[rows, :] = y


def _resident(shape):
    return pl.BlockSpec(shape, lambda i: (0,) * len(shape), pipeline_mode=pl.Buffered(1))


def _resident_layer(stacked, layer):
    return pl.BlockSpec((None,) + stacked.shape[1:], lambda i: (layer, 0, 0), pipeline_mode=pl.Buffered(1))


def _ffn(h, g, wi_all, wo_all, layer, *, proj=None, g_final=None, tm=512, sub=512, tf=256):
    t, d = h.shape
    final = g_final is not None
    rows = lambda width: pl.BlockSpec((tm, width), lambda i: (i, 0))
    in_specs = [rows(d)]
    args = [h]
    if proj is not None:
        a, wp = proj
        in_specs += [rows(a.shape[1]), _resident(wp.shape)]
        args += [a, wp]
    in_specs += [_resident((1, d)), _resident_layer(wi_all, layer), _resident_layer(wo_all, layer)]
    args += [g.reshape(1, d), wi_all, wo_all]
    if final:
        in_specs.append(_resident((1, d)))
        args.append(g_final.reshape(1, d))
    return pl.pallas_call(
        functools.partial(_ffn_body, tf=tf, sub=sub, fused_proj=proj is not None, final=final),
        grid=(t // tm,),
        in_specs=in_specs,
        out_specs=rows(d),
        out_shape=jax.ShapeDtypeStruct((t, d), F32),
        compiler_params=_params("parallel"),
        name="ffn_final" if final else ("ffn_proj" if proj is not None else "ffn"),
    )(*args)


def _norm_matmul_body(x_ref, g_ref, *refs, sub):
    nw = len(refs) // 2
    w_refs, o_refs = refs[:nw], refs[nw:]
    for r in range(x_ref.shape[0] // sub):
        rows = slice(r * sub, (r + 1) * sub)
        x = x_ref[rows, :]
        ms = jnp.mean(x * x, axis=-1, keepdims=True)
        xn = (x * lax.rsqrt(ms + EPS) * g_ref[...]).astype(BF16)
        for w_ref, o_ref in zip(w_refs, o_refs):
            o_ref[rows, :] = _dot(xn, w_ref[...]).astype(o_ref.dtype)


def _norm_matmul(x, g, weights, out_dtypes, *, tm=1024, sub=512, name="norm_matmul"):
    t, k = x.shape
    rows = lambda width: pl.BlockSpec((tm, width), lambda i: (i, 0))
    return pl.pallas_call(
        functools.partial(_norm_matmul_body, sub=sub),
        grid=(t // tm,),
        in_specs=[rows(k), _resident((1, k))] + [_resident(w.shape) for w in weights],
        out_specs=[rows(w.shape[1]) for w in weights],
        out_shape=[jax.ShapeDtypeStruct((t, w.shape[1]), dt) for w, dt in zip(weights, out_dtypes)],
        compiler_params=_params("parallel"),
        name=name,
    )(x, g.reshape(1, k), *weights)


def _sb_attn_body(q_ref, k_ref, v_ref, o_ref, acc_ref, *, tq, hp):
    i = pl.program_id(2)
    tk = tq
    nh = 2 * hp
    lane = lax.broadcasted_iota(jnp.int32, (1, LANES), 1)
    qs = []
    for p in range(hp):
        qp = q_ref[:, p * LANES:(p + 1) * LANES]
        qs.append(jnp.concatenate([jnp.where(lane < SB_HEAD_DIM, qp, jnp.zeros_like(qp)),
                                   jnp.where(lane >= SB_HEAD_DIM, qp, jnp.zeros_like(qp))], axis=0))
    def weights(qst, start, width, c, rows_per_head, masked):
        z = jnp.concatenate(
            [_dot_nt(qst[p], k_ref[pl.ds(start, width), p * LANES:(p + 1) * LANES]) for p in range(hp)],
            axis=0)
        nz = jnp.minimum(z, 0.0)
        pz = z - nz
        soft = jnp.log2(1.0 + jnp.exp2(nz - pz))
        log_beta = nz - soft
        log_fail = log_beta - z
        if masked:
            qrow = lax.broadcasted_iota(jnp.int32, z.shape, 0) & (rows_per_head - 1)
            strict = lax.broadcasted_iota(jnp.int32, z.shape, 1) < qrow
            log_fail = jnp.where(strict, log_fail, 0.0)
            log_beta = jnp.where(strict, log_beta, -jnp.inf)
        later = (lax.broadcasted_iota(jnp.int32, (width, width), 0)
                 > lax.broadcasted_iota(jnp.int32, (width, width), 1)).astype(BF16)
        between = _dot(log_fail.astype(BF16), later) + c
        w = jnp.exp2(log_beta + between).astype(BF16)
        return w, jnp.sum(log_fail, axis=-1, keepdims=True)

    def values(start, width, p):
        return v_ref[pl.ds(start, width), p * LANES:(p + 1) * LANES]

    base = pl.multiple_of(i * tq, tq)
    w, c = weights(qs, base, tk, jnp.zeros((nh * tq, 1), F32), tq, True)
    for p in range(hp):
        acc_ref[p] = _dot(w[2 * p * tq:2 * (p + 1) * tq], values(base, tk, p))

    def cond(carry):
        return jnp.logical_and(carry[0] >= 0, carry[1] > EXP2_UNDERFLOW)

    def body(carry):
        j, _, c = carry
        start = pl.multiple_of(j * tk, tk)
        w, r = weights(qs, start, tk, c, tq, False)
        for p in range(hp):
            acc_ref[p] += _dot(w[2 * p * tq:2 * (p + 1) * tq], values(start, tk, p))
        c = c + r
        return j - 1, jnp.max(c), c

    lax.while_loop(cond, body, (i - 1, jnp.max(c), c))
    for p in range(hp):
        o_ref[:, p * LANES:(p + 1) * LANES] = jnp.where(
            lane < SB_HEAD_DIM, acc_ref[p, 0:tq], acc_ref[p, tq:2 * tq]).astype(o_ref.dtype)


def _sb_attention(qkv, batch, seq, *, tq=256, hp=4):
    t = batch * seq
    groups = SB_HEADS // 2 // hp
    nq = seq // tq
    w = hp * LANES
    return pl.pallas_call(
        functools.partial(_sb_attn_body, tq=tq, hp=hp),
        grid=(batch, groups, nq),
        in_specs=[
            pl.BlockSpec((tq, w), lambda b, g, i: (b * nq + i, g)),
            pl.BlockSpec((seq, w), lambda b, g, i: (b, groups + g)),
            pl.BlockSpec((seq, w), lambda b, g, i: (b, 2 * groups + g)),
        ],
        out_specs=pl.BlockSpec((tq, w), lambda b, g, i: (b * nq + i, g)),
        out_shape=jax.ShapeDtypeStruct((t, groups * w), BF16),
        scratch_shapes=[pltpu.VMEM((hp, 2 * tq, LANES), F32)],
        compiler_params=_params("parallel", "parallel", "arbitrary"),
        name="sb_attention",
    )(qkv, qkv, qkv)


def _mlstm_body(qk_ref, v_ref, op_ref, gcol_ref, grow_ref, cw_ref, bcol_ref, brow_ref, hg_ref,
                o_ref, xs_ref, st_ref, m_ref, *, chunk):
    L = chunk
    c = pl.program_id(1)
    dqk = ML_HEADS * ML_QK_DIM

    @pl.when(c == 0)
    def _():
        xs_ref[0:8, :] = jnp.zeros((8, 2 * dqk), F32)
        st_ref[...] = jnp.zeros_like(st_ref)
        m_ref[...] = jnp.zeros_like(m_ref)

    xs_ref[8:L + 8, :] = qk_ref[...].astype(F32)
    y = cw_ref[0:1, :] * xs_ref[5:L + 5, :]
    for tap in range(1, ML_CONV):
        y = y + cw_ref[tap:tap + 1, :] * xs_ref[5 + tap:L + 5 + tap, :]
    xs_ref[0:8, :] = xs_ref[L:L + 8, :]
    y = y * jax.nn.sigmoid(y)
    q_all = (y[:, :dqk] * (ML_QK_DIM ** -0.5)).astype(BF16)
    k_all = y[:, dqk:].astype(BF16)

    gcol = gcol_ref[...] + bcol_ref[...]
    ti = lax.broadcasted_iota(jnp.int32, (L, L), 0)
    si = lax.broadcasted_iota(jnp.int32, (L, L), 1)
    causal = ti >= si
    tri = causal.astype(BF16)
    bcol = _split_dot_left(tri, _log_sigmoid(gcol))
    grow = grow_ref[0] + brow_ref[...]
    brow = _split_dot(_log_sigmoid(grow), (si >= ti).astype(BF16))

    lane = lax.broadcasted_iota(jnp.int32, (1, LANES), 1)
    srow = lax.broadcasted_iota(jnp.int32, (LANES, 1), 0)
    ones_v = jnp.ones((L, ML_V_DIM), F32)

    for p in range(ML_HEADS // 2):
        qp = q_all[:, p * LANES:(p + 1) * LANES]
        kp = k_all[:, p * LANES:(p + 1) * LANES]
        state = st_ref[p]
        state_b = state.astype(BF16)
        upd = []
        dec = []
        for hh in range(2):
            h = 2 * p + hh
            sel = (lane >= ML_QK_DIM) if hh else (lane < ML_QK_DIM)
            qm = jnp.where(sel, qp, jnp.zeros_like(qp))
            km = jnp.where(sel, kp, jnp.zeros_like(kp))
            vh = v_ref[:, h * ML_V_DIM:(h + 1) * ML_V_DIM]
            b_c = bcol[:, ML_HEADS + h:ML_HEADS + h + 1]
            li_c = gcol[:, h:h + 1]
            b_r = brow[ML_HEADS + h:ML_HEADS + h + 1, :]
            li_r = grow[h:h + 1, :]
            m_prev = m_ref[h:h + 1, 0:1]

            dmat = jnp.where(causal, b_c - b_r + li_r, -jnp.inf)
            inter = b_c + m_prev
            m_t = jnp.maximum(jnp.max(dmat, axis=-1, keepdims=True), inter)
            wt = jnp.exp(dmat - m_t)
            a = jnp.exp(inter - m_t)
            s = _dot_nt(qm, kp) * wt
            qc = _dot(qm, state_b)
            num = _dot(s.astype(BF16), vh) + a * qc[:, :ML_V_DIM]
            den = jnp.sum(s, axis=-1, keepdims=True) + a * qc[:, ML_V_DIM:]
            hout = num / jnp.maximum(jnp.abs(den), jnp.exp(-m_t))

            b_last = b_c[L - 1:L, :]
            g = b_last - b_c + li_c
            m_new = jnp.maximum(b_last + m_prev, jnp.max(g, axis=0, keepdims=True))
            dec.append(jnp.exp(b_last + m_prev - m_new))
            wk = jnp.exp(g - m_new)
            v_aug = jnp.concatenate([vh.astype(F32), ones_v], axis=1)
            upd.append(_dot_tn(km, (wk * v_aug).astype(BF16)))
            m_ref[h:h + 1, :] = jnp.broadcast_to(m_new, (1, LANES))

            ms = jnp.mean(hout * hout, axis=-1, keepdims=True)
            hn = hout * lax.rsqrt(ms + EPS) * hg_ref[:, h * ML_V_DIM:(h + 1) * ML_V_DIM]
            og = jax.nn.sigmoid(op_ref[:, h * ML_V_DIM:(h + 1) * ML_V_DIM].astype(F32))
            o_ref[:, h * ML_V_DIM:(h + 1) * ML_V_DIM] = (og * hn).astype(o_ref.dtype)

        decay = jnp.where(srow < ML_QK_DIM, dec[0], dec[1])
        st_ref[p] = decay * state + upd[0] + upd[1]


def _split_dot_left(m, x):
    hi = x.astype(BF16)
    lo = (x - hi.astype(F32)).astype(BF16)
    return _dot(m, hi) + _dot(m, lo)


def _mlstm(proj, gates, conv_w, b_i, b_f, head_gain, batch, seq, *, chunk=256):
    t = batch * seq
    d = ML_HEADS * ML_V_DIM
    nc = seq // chunk
    grow = gates[:, :2 * ML_HEADS].reshape(batch, seq, 2 * ML_HEADS).transpose(0, 2, 1)
    bias = jnp.concatenate([b_i, b_f]).astype(F32)
    bcol = jnp.zeros((1, LANES), F32).at[0, :2 * ML_HEADS].set(bias)
    brow = bias.reshape(2 * ML_HEADS, 1)
    return pl.pallas_call(
        functools.partial(_mlstm_body, chunk=chunk),
        grid=(batch, nc),
        in_specs=[
            pl.BlockSpec((chunk, d), lambda b, c: (b * nc + c, 0)),
            pl.BlockSpec((chunk, d), lambda b, c: (b * nc + c, 1)),
            pl.BlockSpec((chunk, d), lambda b, c: (b * nc + c, 2)),
            pl.BlockSpec((chunk, LANES), lambda b, c: (b * nc + c, 0)),
            pl.BlockSpec((1, 2 * ML_HEADS, chunk), lambda b, c: (b, 0, c)),
            pl.BlockSpec((ML_CONV, d), lambda b, c: (0, 0)),
            pl.BlockSpec((1, LANES), lambda b, c: (0, 0)),
            pl.BlockSpec((2 * ML_HEADS, 1), lambda b, c: (0, 0)),
            pl.BlockSpec((1, d), lambda b, c: (0, 0)),
        ],
        out_specs=pl.BlockSpec((chunk, d), lambda b, c: (b * nc + c, 0)),
        out_shape=jax.ShapeDtypeStruct((t, d), BF16),
        scratch_shapes=[
            pltpu.VMEM((chunk + 8, d), F32),
            pltpu.VMEM((ML_HEADS // 2, LANES, 2 * ML_V_DIM), F32),
            pltpu.VMEM((ML_HEADS, LANES), F32),
        ],
        compiler_params=_params("parallel", "arbitrary"),
        name="mlstm",
    )(proj, proj, proj, gates, grow, conv_w.astype(F32), bcol, brow, head_gain.reshape(1, d))


def _mla_up_body(c_ref, pos_ref, gq_ref, gkv_ref, wq1_ref, wq2_ref, wk_ref, wv_ref, inv_ref,
                 q_ref, k_ref, v_ref):
    qr, kvr = MLA_Q_RANK, MLA_KV_RANK
    cq = c_ref[:, 0:qr]
    ckv = c_ref[:, 512:512 + kvr]
    kr1 = c_ref[:, 768:896]
    kr2 = c_ref[:, 896:1024]
    cqn = (cq * lax.rsqrt(jnp.mean(cq * cq, axis=-1, keepdims=True) + EPS) * gq_ref[...]).astype(BF16)
    ckvn = (ckv * lax.rsqrt(jnp.mean(ckv * ckv, axis=-1, keepdims=True) + EPS) * gkv_ref[...]).astype(BF16)

    ang = pos_ref[...].astype(F32) * inv_ref[...]
    cosv = jnp.cos(ang)
    sinv = jnp.sin(ang)
    lane = lax.broadcasted_iota(jnp.int32, (1, LANES), 1)
    half = MLA_ROPE // 2
    cpat = jnp.where(lane < MLA_NOPE, 1.0, jnp.where(lane < MLA_NOPE + MLA_ROPE, cosv, 0.0))
    spat = jnp.where(lane < MLA_NOPE, 0.0,
                     jnp.where(lane < MLA_NOPE + half, -sinv,
                               jnp.where(lane < MLA_NOPE + MLA_ROPE, sinv, 0.0)))
    scale = (MLA_NOPE + MLA_ROPE) ** -0.5 * LOG2_E
    cpat_q = cpat * scale
    spat_q = spat * scale

    a1 = _dot(cqn, wq1_ref[...])
    a2 = _dot(cqn, wq2_ref[...])
    kn = _dot(ckvn, wk_ref[...])
    vv = _dot(ckvn, wv_ref[...])
    rk = kr1 * cpat + kr2 * spat
    ones = [jnp.where(lane >= MLA_V, 1.0, 0.0), jnp.where(lane < MLA_V, 1.0, 0.0)]
    for h in range(MLA_HEADS):
        sl = slice(h * LANES, (h + 1) * LANES)
        q_ref[:, sl] = (a1[:, sl] * cpat_q + a2[:, sl] * spat_q).astype(BF16)
        k_ref[:, sl] = (kn[:, sl] + rk).astype(BF16)
        v_ref[:, sl] = (vv[:, sl] + ones[h % 2]).astype(BF16)


def _mla_up(cproj, pos, g_q, g_kv, wq1, wq2, wk, wv, inv_pat, *, tm=512):
    t = cproj.shape[0]
    hw = MLA_HEADS * LANES
    full = lambda shape: pl.BlockSpec(shape, lambda i: (0, 0))
    return pl.pallas_call(
        _mla_up_body,
        grid=(t // tm,),
        in_specs=[
            pl.BlockSpec((tm, cproj.shape[1]), lambda i: (i, 0)),
            pl.BlockSpec((tm, 1), lambda i: (i, 0)),
            full((1, MLA_Q_RANK)), full((1, MLA_KV_RANK)),
            full(wq1.shape), full(wq2.shape), full(wk.shape), full(wv.shape),
            full((1, LANES)),
        ],
        out_specs=[
            pl.BlockSpec((tm, hw), lambda i: (i, 0)),
            pl.BlockSpec((tm, hw), lambda i: (i, 0)),
            pl.BlockSpec((tm, hw), lambda i: (i, 0)),
        ],
        out_shape=[
            jax.ShapeDtypeStruct((t, hw), BF16),
            jax.ShapeDtypeStruct((t, hw), BF16),
            jax.ShapeDtypeStruct((t, hw), BF16),
        ],
        compiler_params=_params("parallel"),
        name="mla_up",
    )(cproj, pos, g_q.reshape(1, -1), g_kv.reshape(1, -1), wq1, wq2, wk, wv, inv_pat)


def _mla_attn_body(q_ref, k_ref, v_ref, o_ref, acc_ref, *, tq, tk, kb, hp):
    i = pl.program_id(2)
    nh = 2 * hp
    nd = tq // tk
    lane = lax.broadcasted_iota(jnp.int32, (1, LANES), 1)
    acc_ref[...] = jnp.zeros_like(acc_ref)
    qs = [q_ref[:, h * LANES:(h + 1) * LANES] for h in range(nh)]

    def run(units, ms):
        ms = list(ms)

        def score(unit):
            h, start, r0 = unit
            return _dot_nt(qs[h][(r0 or 0):], k_ref[pl.ds(start, tk), h * LANES:(h + 1) * LANES])

        ss = [score(u) for u in units[:2]]
        for n, (h, start, r0) in enumerate(units):
            if n + 2 < len(units):
                ss.append(score(units[n + 2]))
            s = ss[n]
            if r0 is not None:
                r = lax.broadcasted_iota(jnp.int32, s.shape, 0)
                cc = lax.broadcasted_iota(jnp.int32, s.shape, 1)
                s = jnp.where(cc <= r, s, -jnp.inf)
            r0 = r0 or 0
            m_old = ms[h][r0:]
            m_new = jnp.maximum(m_old, jnp.max(s, axis=-1, keepdims=True))
            alpha = jnp.exp2(m_old - m_new)
            pr = jnp.exp2(s - m_new)
            vb = v_ref[pl.ds(start, tk), h * LANES:(h + 1) * LANES]
            acc_ref[h, r0:, :] = alpha * acc_ref[h, r0:, :] + _dot(pr.astype(BF16), vb)
            ms[h] = m_new if r0 == 0 else jnp.concatenate([ms[h][:r0], m_new], axis=0)
        return tuple(ms)

    def full_blocks(jb, ms):
        base = jb * (kb * tk)
        return run([(h, pl.multiple_of(base + u * tk, tk), None) for u in range(kb) for h in range(nh)], ms)

    neg = jnp.full((tq, 1), -jnp.inf, F32)
    ms = lax.fori_loop(0, i * (nd // kb), full_blocks, (neg,) * nh)
    base = i * tq
    run([(h, pl.multiple_of(base + d * tk, tk), d * tk) for d in range(nd) for h in range(nh)], ms)
    for p in range(hp):
        r0 = acc_ref[2 * p]
        r1 = acc_ref[2 * p + 1]
        r0 = r0 / pltpu.roll(r0, MLA_V, axis=1)
        r1 = r1 / pltpu.roll(r1, MLA_V, axis=1)
        o_ref[:, p * LANES:(p + 1) * LANES] = jnp.where(lane < MLA_V, r0, r1).astype(o_ref.dtype)


def _mla_attention(q, k, v, batch, seq, *, tq=1024, tk=512, kb=2, hp=1):
    assert tq % (tk * kb) == 0
    t = batch * seq
    groups = MLA_HEADS // 2 // hp
    nq = seq // tq
    w = 2 * hp * LANES
    return pl.pallas_call(
        functools.partial(_mla_attn_body, tq=tq, tk=tk, kb=kb, hp=hp),
        grid=(batch, groups, nq),
        in_specs=[
            pl.BlockSpec((tq, w), lambda b, g, i: (b * nq + i, g)),
            pl.BlockSpec((seq, w), lambda b, g, i: (b, g)),
            pl.BlockSpec((seq, w), lambda b, g, i: (b, g)),
        ],
        out_specs=pl.BlockSpec((tq, hp * LANES), lambda b, g, i: (b * nq + i, g)),
        out_shape=jax.ShapeDtypeStruct((t, groups * hp * LANES), BF16),
        scratch_shapes=[pltpu.VMEM((2 * hp, tq, LANES), F32)],
        compiler_params=_params("parallel", "parallel", "arbitrary"),
        name="mla_attention",
    )(q, k, v)


def _sb_mixer(h, g, w_in, batch, seq):
    d = h.shape[1]
    scale = SB_HEAD_DIM ** -0.5 * LOG2_E
    w = jnp.concatenate([w_in[:, :d] * scale, w_in[:, d:]], axis=1).astype(BF16)
    (qkv,) = _norm_matmul(h, g, [w], [BF16], name="sb_in_proj")
    return _sb_attention(qkv, batch, seq)


def _mlstm_mixer(h, g, w_in, conv_w, b_i, b_f, head_gain, batch, seq):
    d = h.shape[1]
    w_main = w_in[:, :3 * d].astype(BF16)
    w_gate = jnp.zeros((d, LANES), F32).at[:, :2 * ML_HEADS].set(w_in[:, 3 * d:]).astype(BF16)
    proj, gates = _norm_matmul(h, g, [w_main, w_gate], [BF16, F32], name="ml_in_proj")
    return _mlstm(proj, gates, conv_w, b_i, b_f, head_gain, batch, seq)


def _mla_weights(w_in, w_uq, w_ukv):
    d = w_in.shape[0]
    qr, kvr, r = MLA_Q_RANK, MLA_KV_RANK, MLA_ROPE
    half = r // 2
    w_kr = w_in[:, qr + kvr:]
    w_kr_swap = jnp.concatenate([w_kr[:, half:], w_kr[:, :half]], axis=1)
    place = lambda wr: jnp.zeros((d, LANES), F32).at[:, MLA_NOPE:MLA_NOPE + r].set(wr)
    w_c = jnp.concatenate([
        w_in[:, :qr], jnp.zeros((d, 512 - qr), F32),
        w_in[:, qr:qr + kvr], place(w_kr), place(w_kr_swap)], axis=1)

    wq = w_uq.reshape(qr, MLA_HEADS, MLA_NOPE + r)
    q_rope = wq[:, :, MLA_NOPE:]
    q_swap = jnp.concatenate([q_rope[:, :, half:], q_rope[:, :, :half]], axis=2)
    pad = jnp.zeros((qr, MLA_HEADS, LANES - MLA_NOPE - r), F32)
    wq1 = jnp.concatenate([wq, pad], axis=2).reshape(qr, MLA_HEADS * LANES)
    wq2 = jnp.concatenate([jnp.zeros((qr, MLA_HEADS, MLA_NOPE), F32), q_swap, pad], axis=2)
    wq2 = wq2.reshape(qr, MLA_HEADS * LANES)

    wkv = w_ukv.reshape(kvr, MLA_HEADS, MLA_NOPE + MLA_V)
    wk = jnp.concatenate([wkv[:, :, :MLA_NOPE], jnp.zeros((kvr, MLA_HEADS, LANES - MLA_NOPE), F32)], axis=2)
    wk = wk.reshape(kvr, MLA_HEADS * LANES)
    wvp = wkv[:, :, MLA_NOPE:].reshape(kvr, MLA_HEADS // 2, 2, MLA_V)
    zv = jnp.zeros((kvr, MLA_HEADS // 2, MLA_V), F32)
    wv = jnp.stack([jnp.concatenate([wvp[:, :, 0], zv], axis=2),
                    jnp.concatenate([zv, wvp[:, :, 1]], axis=2)], axis=2).reshape(kvr, MLA_HEADS * LANES)
    return w_c.astype(BF16), wq1.astype(BF16), wq2.astype(BF16), wk.astype(BF16), wv.astype(BF16)


def _mla_mixer(h, g, positions, w_in, g_q, w_uq, g_kv, w_ukv, batch, seq):
    w_c, wq1, wq2, wk, wv = _mla_weights(w_in, w_uq, w_ukv)
    (cproj,) = _norm_matmul(h, g, [w_c], [F32], name="mla_in_proj")
    half = MLA_ROPE // 2
    inv = ROPE_THETA ** (-jnp.arange(half, dtype=F32) / half)
    inv_pat = jnp.zeros((1, LANES), F32).at[0, MLA_NOPE:MLA_NOPE + MLA_ROPE].set(jnp.concatenate([inv, inv]))
    pos = positions.reshape(batch * seq, 1)
    q, k, v = _mla_up(cproj, pos, g_q, g_kv, wq1, wq2, wk, wv, inv_pat)
    return _mla_attention(q, k, v, batch, seq)


def kernel(x, positions, ln_ffn1, ffn1_wi, ffn1_wo, ln_mix, ln_ffn2, ffn2_wi, ffn2_wo, sb_w_in, sb_w_out, ml_w_in, ml_conv_w, ml_b_igate, ml_b_fgate, ml_head_gain, ml_w_out, mla_w_in, mla_g_q, mla_w_uq, mla_g_kv, mla_w_ukv, mla_w_out, ln_final):
    batch, seq, d = x.shape
    depth = ln_mix.shape[0]
    h = x.reshape(batch * seq, d)
    wi1, wo1 = ffn1_wi.astype(BF16), ffn1_wo.astype(BF16)
    wi2, wo2 = ffn2_wi.astype(BF16), ffn2_wo.astype(BF16)
    for i in range(depth):
        h = _ffn(h, ln_ffn1[i], wi1, wo1, i)
        j = i // N_MIXERS
        kind = i % N_MIXERS
        if kind == 0:
            o = _sb_mixer(h, ln_mix[i], sb_w_in[j], batch, seq)
            w_out = sb_w_out[j]
        elif kind == 1:
            o = _mlstm_mixer(h, ln_mix[i], ml_w_in[j], ml_conv_w[j], ml_b_igate[j], ml_b_fgate[j],
                             ml_head_gain[j], batch, seq)
            w_out = ml_w_out[j]
        else:
            o = _mla_mixer(h, ln_mix[i], positions, mla_w_in[j], mla_g_q[j], mla_w_uq[j], mla_g_kv[j],
                           mla_w_ukv[j], batch, seq)
            w_out = mla_w_out[j]
        h = _ffn(h, ln_ffn2[i], wi2, wo2, i,
                 proj=(o, w_out.astype(BF16)), g_final=ln_final if i == depth - 1 else None)
    return h.reshape(batch, seq, d)
```

```python
import functools

import jax
import jax.numpy as jnp
from jax import lax
from jax.experimental import pallas as pl
from jax.experimental.pallas import tpu as pltpu

F32 = jnp.float32
BF16 = jnp.bfloat16

EPS = 1e-6
LANES = 128
VMEM_LIMIT = 56 * 1024 * 1024

SB_HEADS = 16
SB_HEAD_DIM = 64
ML_HEADS = 8
ML_QK_DIM = 64
ML_V_DIM = 128
ML_CONV = 4
MLA_HEADS = 16
MLA_NOPE = 64
MLA_ROPE = 32
MLA_V = 64
MLA_Q_RANK = 384
MLA_KV_RANK = 256
ROPE_THETA = 10000.0
LOG2_E = 1.4426950408889634
N_MIXERS = 3

EXP2_UNDERFLOW = -152.0


def _dot(a, b):
    return jnp.dot(a, b, preferred_element_type=F32)


def _dot_nt(a, b):
    return lax.dot_general(a, b, (((1,), (1,)), ((), ())), preferred_element_type=F32)


def _dot_tn(a, b):
    return lax.dot_general(a, b, (((0,), (0,)), ((), ())), preferred_element_type=F32)


def _split_dot(x, m):
    hi = x.astype(BF16)
    lo = (x - hi.astype(F32)).astype(BF16)
    return _dot(hi, m) + _dot(lo, m)


def _log_sigmoid(x):
    return jnp.minimum(x, 0.0) - jnp.log(1.0 + jnp.exp(-jnp.abs(x)))


def _params(*sem):
    return pltpu.CompilerParams(dimension_semantics=sem, vmem_limit_bytes=VMEM_LIMIT)


def _ffn_body(*refs, tf, sub, fused_proj, final):
    refs = list(refs)
    h_ref = refs.pop(0)
    if fused_proj:
        a_ref, wp_ref = refs.pop(0), refs.pop(0)
    g_ref, wi_ref, wo_ref = refs.pop(0), refs.pop(0), refs.pop(0)
    if final:
        gf_ref = refs.pop(0)
    (o_ref,) = refs
    dff = wo_ref.shape[0]
    nf = dff // tf
    for r in range(h_ref.shape[0] // sub):
        rows = slice(r * sub, (r + 1) * sub)
        x = h_ref[rows, :]
        if fused_proj:
            x = x + _dot(a_ref[rows, :], wp_ref[...])
        ms = jnp.mean(x * x, axis=-1, keepdims=True)
        xn = (x * lax.rsqrt(ms + EPS) * g_ref[...]).astype(BF16)

        def gate_up(f):
            return (_dot(xn, wi_ref[:, f * tf:(f + 1) * tf].astype(BF16)),
                    _dot(xn, wi_ref[:, dff + f * tf:dff + (f + 1) * tf].astype(BF16)))

        acc = None
        nxt = gate_up(0)
        for f in range(nf):
            gate, up = nxt
            if f + 1 < nf:
                nxt = gate_up(f + 1)
            act = (gate * jax.nn.sigmoid(gate) * up).astype(BF16)
            part = _dot(act, wo_ref[f * tf:(f + 1) * tf, :].astype(BF16))
            acc = part if acc is None else acc + part
        y = x + 0.5 * acc
        if final:
            ms = jnp.mean(y * y, axis=-1, keepdims=True)
            y = y * lax.rsqrt(ms + EPS) * gf_ref[...]
        o_ref[rows, :] = y


def _resident(shape):
    return pl.BlockSpec(shape, lambda i: (0,) * len(shape), pipeline_mode=pl.Buffered(1))


def _resident_layer(stacked, layer):
    return pl.BlockSpec((None,) + stacked.shape[1:], lambda i: (layer, 0, 0), pipeline_mode=pl.Buffered(1))


def _ffn(h, g, wi_all, wo_all, layer, *, proj=None, g_final=None, tm=512, sub=512, tf=256):
    t, d = h.shape
    final = g_final is not None
    rows = lambda width: pl.BlockSpec((tm, width), lambda i: (i, 0))
    in_specs = [rows(d)]
    args = [h]
    if proj is not None:
        a, wp = proj
        in_specs += [rows(a.shape[1]), _resident(wp.shape)]
        args += [a, wp]
    in_specs += [_resident((1, d)), _resident_layer(wi_all, layer), _resident_layer(wo_all, layer)]
    args += [g.reshape(1, d), wi_all, wo_all]
    if final:
        in_specs.append(_resident((1, d)))
        args.append(g_final.reshape(1, d))
    return pl.pallas_call(
        functools.partial(_ffn_body, tf=tf, sub=sub, fused_proj=proj is not None, final=final),
        grid=(t // tm,),
        in_specs=in_specs,
        out_specs=rows(d),
        out_shape=jax.ShapeDtypeStruct((t, d), F32),
        compiler_params=_params("parallel"),
        name="ffn_final" if final else ("ffn_proj" if proj is not None else "ffn"),
    )(*args)


def _norm_matmul_body(x_ref, g_ref, *refs, sub):
    nw = len(refs) // 2
    w_refs, o_refs = refs[:nw], refs[nw:]
    for r in range(x_ref.shape[0] // sub):
        rows = slice(r * sub, (r + 1) * sub)
        x = x_ref[rows, :]
        ms = jnp.mean(x * x, axis=-1, keepdims=True)
        xn = (x * lax.rsqrt(ms + EPS) * g_ref[...]).astype(BF16)
        for w_ref, o_ref in zip(w_refs, o_refs):
            o_ref[rows, :] = _dot(xn, w_ref[...]).astype(o_ref.dtype)


def _norm_matmul(x, g, weights, out_dtypes, *, tm=1024, sub=512, name="norm_matmul"):
    t, k = x.shape
    rows = lambda width: pl.BlockSpec((tm, width), lambda i: (i, 0))
    return pl.pallas_call(
        functools.partial(_norm_matmul_body, sub=sub),
        grid=(t // tm,),
        in_specs=[rows(k), _resident((1, k))] + [_resident(w.shape) for w in weights],
        out_specs=[rows(w.shape[1]) for w in weights],
        out_shape=[jax.ShapeDtypeStruct((t, w.shape[1]), dt) for w, dt in zip(weights, out_dtypes)],
        compiler_params=_params("parallel"),
        name=name,
    )(x, g.reshape(1, k), *weights)


def _sb_attn_body(q_ref, k_ref, v_ref, o_ref, acc_ref, *, tq, hp):
    i = pl.program_id(2)
    tk = tq
    nh = 2 * hp
    lane = lax.broadcasted_iota(jnp.int32, (1, LANES), 1)
    qs = []
    for p in range(hp):
        qp = q_ref[:, p * LANES:(p + 1) * LANES]
        qs.append(jnp.concatenate([jnp.where(lane < SB_HEAD_DIM, qp, jnp.zeros_like(qp)),
                                   jnp.where(lane >= SB_HEAD_DIM, qp, jnp.zeros_like(qp))], axis=0))
    def weights(qst, start, width, c, rows_per_head, masked):
        z = jnp.concatenate(
            [_dot_nt(qst[p], k_ref[pl.ds(start, width), p * LANES:(p + 1) * LANES]) for p in range(hp)],
            axis=0)
        nz = jnp.minimum(z, 0.0)
        pz = z - nz
        soft = jnp.log2(1.0 + jnp.exp2(nz - pz))
        log_beta = nz - soft
        log_fail = log_beta - z
        if masked:
            qrow = lax.broadcasted_iota(jnp.int32, z.shape, 0) & (rows_per_head - 1)
            strict = lax.broadcasted_iota(jnp.int32, z.shape, 1) < qrow
            log_fail = jnp.where(strict, log_fail, 0.0)
            log_beta = jnp.where(strict, log_beta, -jnp.inf)
        later = (lax.broadcasted_iota(jnp.int32, (width, width), 0)
                 > lax.broadcasted_iota(jnp.int32, (width, width), 1)).astype(BF16)
        between = _dot(log_fail.astype(BF16), later) + c
        w = jnp.exp2(log_beta + between).astype(BF16)
        return w, jnp.sum(log_fail, axis=-1, keepdims=True)

    def values(start, width, p):
        return v_ref[pl.ds(start, width), p * LANES:(p + 1) * LANES]

    base = pl.multiple_of(i * tq, tq)
    w, c = weights(qs, base, tk, jnp.zeros((nh * tq, 1), F32), tq, True)
    for p in range(hp):
        acc_ref[p] = _dot(w[2 * p * tq:2 * (p + 1) * tq], values(base, tk, p))

    def cond(carry):
        return jnp.logical_and(carry[0] >= 0, carry[1] > EXP2_UNDERFLOW)

    def body(carry):
        j, _, c = carry
        start = pl.multiple_of(j * tk, tk)
        w, r = weights(qs, start, tk, c, tq, False)
        for p in range(hp):
            acc_ref[p] += _dot(w[2 * p * tq:2 * (p + 1) * tq], values(start, tk, p))
        c = c + r
        return j - 1, jnp.max(c), c

    lax.while_loop(cond, body, (i - 1, jnp.max(c), c))
    for p in range(hp):
        o_ref[:, p * LANES:(p + 1) * LANES] = jnp.where(
            lane < SB_HEAD_DIM, acc_ref[p, 0:tq], acc_ref[p, tq:2 * tq]).astype(o_ref.dtype)


def _sb_attention(qkv, batch, seq, *, tq=256, hp=4):
    t = batch * seq
    groups = SB_HEADS // 2 // hp
    nq = seq // tq
    w = hp * LANES
    return pl.pallas_call(
        functools.partial(_sb_attn_body, tq=tq, hp=hp),
        grid=(batch, groups, nq),
        in_specs=[
            pl.BlockSpec((tq, w), lambda b, g, i: (b * nq + i, g)),
            pl.BlockSpec((seq, w), lambda b, g, i: (b, groups + g)),
            pl.BlockSpec((seq, w), lambda b, g, i: (b, 2 * groups + g)),
        ],
        out_specs=pl.BlockSpec((tq, w), lambda b, g, i: (b * nq + i, g)),
        out_shape=jax.ShapeDtypeStruct((t, groups * w), BF16),
        scratch_shapes=[pltpu.VMEM((hp, 2 * tq, LANES), F32)],
        compiler_params=_params("parallel", "parallel", "arbitrary"),
        name="sb_attention",
    )(qkv, qkv, qkv)


def _mlstm_body(qk_ref, v_ref, op_ref, gcol_ref, grow_ref, cw_ref, bcol_ref, brow_ref, hg_ref,
                o_ref, xs_ref, st_ref, m_ref, *, chunk):
    L = chunk
    c = pl.program_id(1)
    dqk = ML_HEADS * ML_QK_DIM

    @pl.when(c == 0)
    def _():
        xs_ref[0:8, :] = jnp.zeros((8, 2 * dqk), F32)
        st_ref[...] = jnp.zeros_like(st_ref)
        m_ref[...] = jnp.zeros_like(m_ref)

    xs_ref[8:L + 8, :] = qk_ref[...].astype(F32)
    y = cw_ref[0:1, :] * xs_ref[5:L + 5, :]
    for tap in range(1, ML_CONV):
        y = y + cw_ref[tap:tap + 1, :] * xs_ref[5 + tap:L + 5 + tap, :]
    xs_ref[0:8, :] = xs_ref[L:L + 8, :]
    y = y * jax.nn.sigmoid(y)
    q_all = (y[:, :dqk] * (ML_QK_DIM ** -0.5)).astype(BF16)
    k_all = y[:, dqk:].astype(BF16)

    gcol = gcol_ref[...] + bcol_ref[...]
    ti = lax.broadcasted_iota(jnp.int32, (L, L), 0)
    si = lax.broadcasted_iota(jnp.int32, (L, L), 1)
    causal = ti >= si
    tri = causal.astype(BF16)
    bcol = _split_dot_left(tri, _log_sigmoid(gcol))
    grow = grow_ref[0] + brow_ref[...]
    brow = _split_dot(_log_sigmoid(grow), (si >= ti).astype(BF16))

    lane = lax.broadcasted_iota(jnp.int32, (1, LANES), 1)
    srow = lax.broadcasted_iota(jnp.int32, (LANES, 1), 0)
    ones_v = jnp.ones((L, ML_V_DIM), F32)

    for p in range(ML_HEADS // 2):
        qp = q_all[:, p * LANES:(p + 1) * LANES]
        kp = k_all[:, p * LANES:(p + 1) * LANES]
        state = st_ref[p]
        state_b = state.astype(BF16)
        upd = []
        dec = []
        for hh in range(2):
            h = 2 * p + hh
            sel = (lane >= ML_QK_DIM) if hh else (lane < ML_QK_DIM)
            qm = jnp.where(sel, qp, jnp.zeros_like(qp))
            km = jnp.where(sel, kp, jnp.zeros_like(kp))
            vh = v_ref[:, h * ML_V_DIM:(h + 1) * ML_V_DIM]
            b_c = bcol[:, ML_HEADS + h:ML_HEADS + h + 1]
            li_c = gcol[:, h:h + 1]
            b_r = brow[ML_HEADS + h:ML_HEADS + h + 1, :]
            li_r = grow[h:h + 1, :]
            m_prev = m_ref[h:h + 1, 0:1]

            dmat = jnp.where(causal, b_c - b_r + li_r, -jnp.inf)
            inter = b_c + m_prev
            m_t = jnp.maximum(jnp.max(dmat, axis=-1, keepdims=True), inter)
            wt = jnp.exp(dmat - m_t)
            a = jnp.exp(inter - m_t)
            s = _dot_nt(qm, kp) * wt
            qc = _dot(qm, state_b)
            num = _dot(s.astype(BF16), vh) + a * qc[:, :ML_V_DIM]
            den = jnp.sum(s, axis=-1, keepdims=True) + a * qc[:, ML_V_DIM:]
            hout = num / jnp.maximum(jnp.abs(den), jnp.exp(-m_t))

            b_last = b_c[L - 1:L, :]
            g = b_last - b_c + li_c
            m_new = jnp.maximum(b_last + m_prev, jnp.max(g, axis=0, keepdims=True))
            dec.append(jnp.exp(b_last + m_prev - m_new))
            wk = jnp.exp(g - m_new)
            v_aug = jnp.concatenate([vh.astype(F32), ones_v], axis=1)
            upd.append(_dot_tn(km, (wk * v_aug).astype(BF16)))
            m_ref[h:h + 1, :] = jnp.broadcast_to(m_new, (1, LANES))

            ms = jnp.mean(hout * hout, axis=-1, keepdims=True)
            hn = hout * lax.rsqrt(ms + EPS) * hg_ref[:, h * ML_V_DIM:(h + 1) * ML_V_DIM]
            og = jax.nn.sigmoid(op_ref[:, h * ML_V_DIM:(h + 1) * ML_V_DIM].astype(F32))
            o_ref[:, h * ML_V_DIM:(h + 1) * ML_V_DIM] = (og * hn).astype(o_ref.dtype)

        decay = jnp.where(srow < ML_QK_DIM, dec[0], dec[1])
        st_ref[p] = decay * state + upd[0] + upd[1]


def _split_dot_left(m, x):
    hi = x.astype(BF16)
    lo = (x - hi.astype(F32)).astype(BF16)
    return _dot(m, hi) + _dot(m, lo)


def _mlstm(proj, gates, conv_w, b_i, b_f, head_gain, batch, seq, *, chunk=256):
    t = batch * seq
    d = ML_HEADS * ML_V_DIM
    nc = seq // chunk
    grow = gates[:, :2 * ML_HEADS].reshape(batch, seq, 2 * ML_HEADS).transpose(0, 2, 1)
    bias = jnp.concatenate([b_i, b_f]).astype(F32)
    bcol = jnp.zeros((1, LANES), F32).at[0, :2 * ML_HEADS].set(bias)
    brow = bias.reshape(2 * ML_HEADS, 1)
    return pl.pallas_call(
        functools.partial(_mlstm_body, chunk=chunk),
        grid=(batch, nc),
        in_specs=[
            pl.BlockSpec((chunk, d), lambda b, c: (b * nc + c, 0)),
            pl.BlockSpec((chunk, d), lambda b, c: (b * nc + c, 1)),
            pl.BlockSpec((chunk, d), lambda b, c: (b * nc + c, 2)),
            pl.BlockSpec((chunk, LANES), lambda b, c: (b * nc + c, 0)),
            pl.BlockSpec((1, 2 * ML_HEADS, chunk), lambda b, c: (b, 0, c)),
            pl.BlockSpec((ML_CONV, d), lambda b, c: (0, 0)),
            pl.BlockSpec((1, LANES), lambda b, c: (0, 0)),
            pl.BlockSpec((2 * ML_HEADS, 1), lambda b, c: (0, 0)),
            pl.BlockSpec((1, d), lambda b, c: (0, 0)),
        ],
        out_specs=pl.BlockSpec((chunk, d), lambda b, c: (b * nc + c, 0)),
        out_shape=jax.ShapeDtypeStruct((t, d), BF16),
        scratch_shapes=[
            pltpu.VMEM((chunk + 8, d), F32),
            pltpu.VMEM((ML_HEADS // 2, LANES, 2 * ML_V_DIM), F32),
            pltpu.VMEM((ML_HEADS, LANES), F32),
        ],
        compiler_params=_params("parallel", "arbitrary"),
        name="mlstm",
    )(proj, proj, proj, gates, grow, conv_w.astype(F32), bcol, brow, head_gain.reshape(1, d))


def _mla_up_body(c_ref, pos_ref, gq_ref, gkv_ref, wq1_ref, wq2_ref, wk_ref, wv_ref, inv_ref,
                 q_ref, k_ref, v_ref):
    qr, kvr = MLA_Q_RANK, MLA_KV_RANK
    cq = c_ref[:, 0:qr]
    ckv = c_ref[:, 512:512 + kvr]
    kr1 = c_ref[:, 768:896]
    kr2 = c_ref[:, 896:1024]
    cqn = (cq * lax.rsqrt(jnp.mean(cq * cq, axis=-1, keepdims=True) + EPS) * gq_ref[...]).astype(BF16)
    ckvn = (ckv * lax.rsqrt(jnp.mean(ckv * ckv, axis=-1, keepdims=True) + EPS) * gkv_ref[...]).astype(BF16)

    ang = pos_ref[...].astype(F32) * inv_ref[...]
    cosv = jnp.cos(ang)
    sinv = jnp.sin(ang)
    lane = lax.broadcasted_iota(jnp.int32, (1, LANES), 1)
    half = MLA_ROPE // 2
    cpat = jnp.where(lane < MLA_NOPE, 1.0, jnp.where(lane < MLA_NOPE + MLA_ROPE, cosv, 0.0))
    spat = jnp.where(lane < MLA_NOPE, 0.0,
                     jnp.where(lane < MLA_NOPE + half, -sinv,
                               jnp.where(lane < MLA_NOPE + MLA_ROPE, sinv, 0.0)))
    scale = (MLA_NOPE + MLA_ROPE) ** -0.5 * LOG2_E
    cpat_q = cpat * scale
    spat_q = spat * scale

    a1 = _dot(cqn, wq1_ref[...])
    a2 = _dot(cqn, wq2_ref[...])
    kn = _dot(ckvn, wk_ref[...])
    vv = _dot(ckvn, wv_ref[...])
    rk = kr1 * cpat + kr2 * spat
    ones = [jnp.where(lane >= MLA_V, 1.0, 0.0), jnp.where(lane < MLA_V, 1.0, 0.0)]
    for h in range(MLA_HEADS):
        sl = slice(h * LANES, (h + 1) * LANES)
        q_ref[:, sl] = (a1[:, sl] * cpat_q + a2[:, sl] * spat_q).astype(BF16)
        k_ref[:, sl] = (kn[:, sl] + rk).astype(BF16)
        v_ref[:, sl] = (vv[:, sl] + ones[h % 2]).astype(BF16)


def _mla_up(cproj, pos, g_q, g_kv, wq1, wq2, wk, wv, inv_pat, *, tm=512):
    t = cproj.shape[0]
    hw = MLA_HEADS * LANES
    full = lambda shape: pl.BlockSpec(shape, lambda i: (0, 0))
    return pl.pallas_call(
        _mla_up_body,
        grid=(t // tm,),
        in_specs=[
            pl.BlockSpec((tm, cproj.shape[1]), lambda i: (i, 0)),
            pl.BlockSpec((tm, 1), lambda i: (i, 0)),
            full((1, MLA_Q_RANK)), full((1, MLA_KV_RANK)),
            full(wq1.shape), full(wq2.shape), full(wk.shape), full(wv.shape),
            full((1, LANES)),
        ],
        out_specs=[
            pl.BlockSpec((tm, hw), lambda i: (i, 0)),
            pl.BlockSpec((tm, hw), lambda i: (i, 0)),
            pl.BlockSpec((tm, hw), lambda i: (i, 0)),
        ],
        out_shape=[
            jax.ShapeDtypeStruct((t, hw), BF16),
            jax.ShapeDtypeStruct((t, hw), BF16),
            jax.ShapeDtypeStruct((t, hw), BF16),
        ],
        compiler_params=_params("parallel"),
        name="mla_up",
    )(cproj, pos, g_q.reshape(1, -1), g_kv.reshape(1, -1), wq1, wq2, wk, wv, inv_pat)


def _mla_attn_body(q_ref, k_ref, v_ref, o_ref, acc_ref, *, tq, tk, kb, hp):
    i = pl.program_id(2)
    nh = 2 * hp
    nd = tq // tk
    lane = lax.broadcasted_iota(jnp.int32, (1, LANES), 1)
    acc_ref[...] = jnp.zeros_like(acc_ref)
    qs = [q_ref[:, h * LANES:(h + 1) * LANES] for h in range(nh)]

    def run(units, ms):
        ms = list(ms)

        def score(unit):
            h, start, r0 = unit
            return _dot_nt(qs[h][(r0 or 0):], k_ref[pl.ds(start, tk), h * LANES:(h + 1) * LANES])

        ss = [score(u) for u in units[:2]]
        for n, (h, start, r0) in enumerate(units):
            if n + 2 < len(units):
                ss.append(score(units[n + 2]))
            s = ss[n]
            if r0 is not None:
                r = lax.broadcasted_iota(jnp.int32, s.shape, 0)
                cc = lax.broadcasted_iota(jnp.int32, s.shape, 1)
                s = jnp.where(cc <= r, s, -jnp.inf)
            r0 = r0 or 0
            m_old = ms[h][r0:]
            m_new = jnp.maximum(m_old, jnp.max(s, axis=-1, keepdims=True))
            alpha = jnp.exp2(m_old - m_new)
            pr = jnp.exp2(s - m_new)
            vb = v_ref[pl.ds(start, tk), h * LANES:(h + 1) * LANES]
            acc_ref[h, r0:, :] = alpha * acc_ref[h, r0:, :] + _dot(pr.astype(BF16), vb)
            ms[h] = m_new if r0 == 0 else jnp.concatenate([ms[h][:r0], m_new], axis=0)
        return tuple(ms)

    def full_blocks(jb, ms):
        base = jb * (kb * tk)
        return run([(h, pl.multiple_of(base + u * tk, tk), None) for u in range(kb) for h in range(nh)], ms)

    neg = jnp.full((tq, 1), -jnp.inf, F32)
    ms = lax.fori_loop(0, i * (nd // kb), full_blocks, (neg,) * nh)
    base = i * tq
    run([(h, pl.multiple_of(base + d * tk, tk), d * tk) for d in range(nd) for h in range(nh)], ms)
    for p in range(hp):
        r0 = acc_ref[2 * p]
        r1 = acc_ref[2 * p + 1]
        r0 = r0 / pltpu.roll(r0, MLA_V, axis=1)
        r1 = r1 / pltpu.roll(r1, MLA_V, axis=1)
        o_ref[:, p * LANES:(p + 1) * LANES] = jnp.where(lane < MLA_V, r0, r1).astype(o_ref.dtype)


def _mla_attention(q, k, v, batch, seq, *, tq=1024, tk=512, kb=2, hp=1):
    assert tq % (tk * kb) == 0
    t = batch * seq
    groups = MLA_HEADS // 2 // hp
    nq = seq // tq
    w = 2 * hp * LANES
    return pl.pallas_call(
        functools.partial(_mla_attn_body, tq=tq, tk=tk, kb=kb, hp=hp),
        grid=(batch, groups, nq),
        in_specs=[
            pl.BlockSpec((tq, w), lambda b, g, i: (b * nq + i, g)),
            pl.BlockSpec((seq, w), lambda b, g, i: (b, g)),
            pl.BlockSpec((seq, w), lambda b, g, i: (b, g)),
        ],
        out_specs=pl.BlockSpec((tq, hp * LANES), lambda b, g, i: (b * nq + i, g)),
        out_shape=jax.ShapeDtypeStruct((t, groups * hp * LANES), BF16),
        scratch_shapes=[pltpu.VMEM((2 * hp, tq, LANES), F32)],
        compiler_params=_params("parallel", "parallel", "arbitrary"),
        name="mla_attention",
    )(q, k, v)


def _sb_mixer(h, g, w_in, batch, seq):
    d = h.shape[1]
    scale = SB_HEAD_DIM ** -0.5 * LOG2_E
    w = jnp.concatenate([w_in[:, :d] * scale, w_in[:, d:]], axis=1).astype(BF16)
    (qkv,) = _norm_matmul(h, g, [w], [BF16], name="sb_in_proj")
    return _sb_attention(qkv, batch, seq)


def _mlstm_mixer(h, g, w_in, conv_w, b_i, b_f, head_gain, batch, seq):
    d = h.shape[1]
    w_main = w_in[:, :3 * d].astype(BF16)
    w_gate = jnp.zeros((d, LANES), F32).at[:, :2 * ML_HEADS].set(w_in[:, 3 * d:]).astype(BF16)
    proj, gates = _norm_matmul(h, g, [w_main, w_gate], [BF16, F32], name="ml_in_proj")
    return _mlstm(proj, gates, conv_w, b_i, b_f, head_gain, batch, seq)


def _mla_weights(w_in, w_uq, w_ukv):
    d = w_in.shape[0]
    qr, kvr, r = MLA_Q_RANK, MLA_KV_RANK, MLA_ROPE
    half = r // 2
    w_kr = w_in[:, qr + kvr:]
    w_kr_swap = jnp.concatenate([w_kr[:, half:], w_kr[:, :half]], axis=1)
    place = lambda wr: jnp.zeros((d, LANES), F32).at[:, MLA_NOPE:MLA_NOPE + r].set(wr)
    w_c = jnp.concatenate([
        w_in[:, :qr], jnp.zeros((d, 512 - qr), F32),
        w_in[:, qr:qr + kvr], place(w_kr), place(w_kr_swap)], axis=1)

    wq = w_uq.reshape(qr, MLA_HEADS, MLA_NOPE + r)
    q_rope = wq[:, :, MLA_NOPE:]
    q_swap = jnp.concatenate([q_rope[:, :, half:], q_rope[:, :, :half]], axis=2)
    pad = jnp.zeros((qr, MLA_HEADS, LANES - MLA_NOPE - r), F32)
    wq1 = jnp.concatenate([wq, pad], axis=2).reshape(qr, MLA_HEADS * LANES)
    wq2 = jnp.concatenate([jnp.zeros((qr, MLA_HEADS, MLA_NOPE), F32), q_swap, pad], axis=2)
    wq2 = wq2.reshape(qr, MLA_HEADS * LANES)

    wkv = w_ukv.reshape(kvr, MLA_HEADS, MLA_NOPE + MLA_V)
    wk = jnp.concatenate([wkv[:, :, :MLA_NOPE], jnp.zeros((kvr, MLA_HEADS, LANES - MLA_NOPE), F32)], axis=2)
    wk = wk.reshape(kvr, MLA_HEADS * LANES)
    wvp = wkv[:, :, MLA_NOPE:].reshape(kvr, MLA_HEADS // 2, 2, MLA_V)
    zv = jnp.zeros((kvr, MLA_HEADS // 2, MLA_V), F32)
    wv = jnp.stack([jnp.concatenate([wvp[:, :, 0], zv], axis=2),
                    jnp.concatenate([zv, wvp[:, :, 1]], axis=2)], axis=2).reshape(kvr, MLA_HEADS * LANES)
    return w_c.astype(BF16), wq1.astype(BF16), wq2.astype(BF16), wk.astype(BF16), wv.astype(BF16)


def _mla_mixer(h, g, positions, w_in, g_q, w_uq, g_kv, w_ukv, batch, seq):
    w_c, wq1, wq2, wk, wv = _mla_weights(w_in, w_uq, w_ukv)
    (cproj,) = _norm_matmul(h, g, [w_c], [F32], name="mla_in_proj")
    half = MLA_ROPE // 2
    inv = ROPE_THETA ** (-jnp.arange(half, dtype=F32) / half)
    inv_pat = jnp.zeros((1, LANES), F32).at[0, MLA_NOPE:MLA_NOPE + MLA_ROPE].set(jnp.concatenate([inv, inv]))
    pos = positions.reshape(batch * seq, 1)
    q, k, v = _mla_up(cproj, pos, g_q, g_kv, wq1, wq2, wk, wv, inv_pat)
    return _mla_attention(q, k, v, batch, seq)


def kernel(x, positions, ln_ffn1, ffn1_wi, ffn1_wo, ln_mix, ln_ffn2, ffn2_wi, ffn2_wo, sb_w_in, sb_w_out, ml_w_in, ml_conv_w, ml_b_igate, ml_b_fgate, ml_head_gain, ml_w_out, mla_w_in, mla_g_q, mla_w_uq, mla_g_kv, mla_w_ukv, mla_w_out, ln_final):
    batch, seq, d = x.shape
    depth = ln_mix.shape[0]
    h = x.reshape(batch * seq, d)
    wi1, wo1, wi2, wo2 = ffn1_wi, ffn1_wo, ffn2_wi, ffn2_wo
    for i in range(depth):
        h = _ffn(h, ln_ffn1[i], wi1, wo1, i)
        j = i // N_MIXERS
        kind = i % N_MIXERS
        if kind == 0:
            o = _sb_mixer(h, ln_mix[i], sb_w_in[j], batch, seq)
            w_out = sb_w_out[j]
        elif kind == 1:
            o = _mlstm_mixer(h, ln_mix[i], ml_w_in[j], ml_conv_w[j], ml_b_igate[j], ml_b_fgate[j],
                             ml_head_gain[j], batch, seq)
            w_out = ml_w_out[j]
        else:
            o = _mla_mixer(h, ln_mix[i], positions, mla_w_in[j], mla_g_q[j], mla_w_uq[j], mla_g_kv[j],
                           mla_w_ukv[j], batch, seq)
            w_out = mla_w_out[j]
        h = _ffn(h, ln_ffn2[i], wi2, wo2, i,
                 proj=(o, w_out.astype(BF16)), g_final=ln_final if i == depth - 1 else None)
    return h.reshape(batch, seq, d)
```

```python
import functools

import jax
import jax.numpy as jnp
from jax import lax
from jax.experimental import pallas as pl
from jax.experimental.pallas import tpu as pltpu

F32 = jnp.float32
BF16 = jnp.bfloat16

EPS = 1e-6
LANES = 128
VMEM_LIMIT = 56 * 1024 * 1024

SB_HEADS = 16
SB_HEAD_DIM = 64
ML_HEADS = 8
ML_QK_DIM = 64
ML_V_DIM = 128
ML_CONV = 4
MLA_HEADS = 16
MLA_NOPE = 64
MLA_ROPE = 32
MLA_V = 64
MLA_Q_RANK = 384
MLA_KV_RANK = 256
ROPE_THETA = 10000.0
LOG2_E = 1.4426950408889634
N_MIXERS = 3

EXP2_UNDERFLOW = -152.0


def _dot(a, b):
    return jnp.dot(a, b, preferred_element_type=F32)


def _dot_nt(a, b):
    return lax.dot_general(a, b, (((1,), (1,)), ((), ())), preferred_element_type=F32)


def _dot_tn(a, b):
    return lax.dot_general(a, b, (((0,), (0,)), ((), ())), preferred_element_type=F32)


def _split_dot(x, m):
    hi = x.astype(BF16)
    lo = (x - hi.astype(F32)).astype(BF16)
    return _dot(hi, m) + _dot(lo, m)


def _log_sigmoid(x):
    return jnp.minimum(x, 0.0) - jnp.log(1.0 + jnp.exp(-jnp.abs(x)))


def _params(*sem):
    return pltpu.CompilerParams(dimension_semantics=sem, vmem_limit_bytes=VMEM_LIMIT)


def _ffn_body(*refs, tf, sub, fused_proj, final):
    refs = list(refs)
    h_ref = refs.pop(0)
    if fused_proj:
        a_ref, wp_ref = refs.pop(0), refs.pop(0)
    g_ref, wi_ref, wo_ref = refs.pop(0), refs.pop(0), refs.pop(0)
    if final:
        gf_ref = refs.pop(0)
    (o_ref,) = refs
    dff = wo_ref.shape[0]
    nf = dff // tf
    for r in range(h_ref.shape[0] // sub):
        rows = slice(r * sub, (r + 1) * sub)
        x = h_ref[rows, :]
        if fused_proj:
            x = x + _dot(a_ref[rows, :], wp_ref[...])
        ms = jnp.mean(x * x, axis=-1, keepdims=True)
        xn = (x * lax.rsqrt(ms + EPS) * g_ref[...]).astype(BF16)

        def gate_up(f):
            return (_dot(xn, wi_ref[:, f * tf:(f + 1) * tf].astype(BF16)),
                    _dot(xn, wi_ref[:, dff + f * tf:dff + (f + 1) * tf].astype(BF16)))

        acc = None
        nxt = gate_up(0)
        for f in range(nf):
            gate, up = nxt
            if f + 1 < nf:
                nxt = gate_up(f + 1)
            act = (gate * jax.nn.sigmoid(gate) * up).astype(BF16)
            part = _dot(act, wo_ref[f * tf:(f + 1) * tf, :].astype(BF16))
            acc = part if acc is None else acc + part
        y = x + 0.5 * acc
        if final:
            ms = jnp.mean(y * y, axis=-1, keepdims=True)
            y = y * lax.rsqrt(ms + EPS) * gf_ref[...]
        o_ref[rows, :] = y


def _resident(shape):
    return pl.BlockSpec(shape, lambda i: (0,) * len(shape), pipeline_mode=pl.Buffered(1))


def _resident_layer(stacked, layer):
    return pl.BlockSpec((None,) + stacked.shape[1:], lambda i: (layer, 0, 0), pipeline_mode=pl.Buffered(1))


def _ffn(h, g, wi_all, wo_all, layer, *, proj=None, g_final=None, tm=512, sub=512, tf=256):
    t, d = h.shape
    final = g_final is not None
    rows = lambda width: pl.BlockSpec((tm, width), lambda i: (i, 0))
    in_specs = [rows(d)]
    args = [h]
    if proj is not None:
        a, wp = proj
        in_specs += [rows(a.shape[1]), _resident(wp.shape)]
        args += [a, wp]
    in_specs += [_resident((1, d)), _resident_layer(wi_all, layer), _resident_layer(wo_all, layer)]
    args += [g.reshape(1, d), wi_all, wo_all]
    if final:
        in_specs.append(_resident((1, d)))
        args.append(g_final.reshape(1, d))
    return pl.pallas_call(
        functools.partial(_ffn_body, tf=tf, sub=sub, fused_proj=proj is not None, final=final),
        grid=(t // tm,),
        in_specs=in_specs,
        out_specs=rows(d),
        out_shape=jax.ShapeDtypeStruct((t, d), F32),
        compiler_params=_params("parallel"),
        name="ffn_final" if final else ("ffn_proj" if proj is not None else "ffn"),
    )(*args)


def _norm_matmul_body(x_ref, g_ref, *refs, sub):
    nw = len(refs) // 2
    w_refs, o_refs = refs[:nw], refs[nw:]
    for r in range(x_ref.shape[0] // sub):
        rows = slice(r * sub, (r + 1) * sub)
        x = x_ref[rows, :]
        ms = jnp.mean(x * x, axis=-1, keepdims=True)
        xn = (x * lax.rsqrt(ms + EPS) * g_ref[...]).astype(BF16)
        for w_ref, o_ref in zip(w_refs, o_refs):
            o_ref[rows, :] = _dot(xn, w_ref[...]).astype(o_ref.dtype)


def _norm_matmul(x, g, weights, out_dtypes, *, tm=1024, sub=512, name="norm_matmul"):
    t, k = x.shape
    rows = lambda width: pl.BlockSpec((tm, width), lambda i: (i, 0))
    return pl.pallas_call(
        functools.partial(_norm_matmul_body, sub=sub),
        grid=(t // tm,),
        in_specs=[rows(k), _resident((1, k))] + [_resident(w.shape) for w in weights],
        out_specs=[rows(w.shape[1]) for w in weights],
        out_shape=[jax.ShapeDtypeStruct((t, w.shape[1]), dt) for w, dt in zip(weights, out_dtypes)],
        compiler_params=_params("parallel"),
        name=name,
    )(x, g.reshape(1, k), *weights)


def _sb_attn_body(q_ref, k_ref, v_ref, o_ref, acc_ref, *, tq, hp):
    i = pl.program_id(2)
    tk = tq
    nh = 2 * hp
    lane = lax.broadcasted_iota(jnp.int32, (1, LANES), 1)
    qs = []
    for p in range(hp):
        qp = q_ref[:, p * LANES:(p + 1) * LANES]
        qs.append(jnp.concatenate([jnp.where(lane < SB_HEAD_DIM, qp, jnp.zeros_like(qp)),
                                   jnp.where(lane >= SB_HEAD_DIM, qp, jnp.zeros_like(qp))], axis=0))
    def weights(qst, start, width, c, rows_per_head, masked):
        z = jnp.concatenate(
            [_dot_nt(qst[p], k_ref[pl.ds(start, width), p * LANES:(p + 1) * LANES]) for p in range(hp)],
            axis=0)
        nz = jnp.minimum(z, 0.0)
        pz = z - nz
        soft = jnp.log2(1.0 + jnp.exp2(nz - pz))
        log_beta = nz - soft
        log_fail = log_beta - z
        if masked:
            qrow = lax.broadcasted_iota(jnp.int32, z.shape, 0) & (rows_per_head - 1)
            strict = lax.broadcasted_iota(jnp.int32, z.shape, 1) < qrow
            log_fail = jnp.where(strict, log_fail, 0.0)
            log_beta = jnp.where(strict, log_beta, -jnp.inf)
        later = (lax.broadcasted_iota(jnp.int32, (width, width), 0)
                 > lax.broadcasted_iota(jnp.int32, (width, width), 1)).astype(BF16)
        between = _dot(log_fail.astype(BF16), later) + c
        w = jnp.exp2(log_beta + between).astype(BF16)
        return w, jnp.sum(log_fail, axis=-1, keepdims=True)

    def values(start, width, p):
        return v_ref[pl.ds(start, width), p * LANES:(p + 1) * LANES]

    base = pl.multiple_of(i * tq, tq)
    w, c = weights(qs, base, tk, jnp.zeros((nh * tq, 1), F32), tq, True)
    for p in range(hp):
        acc_ref[p] = _dot(w[2 * p * tq:2 * (p + 1) * tq], values(base, tk, p))

    def cond(carry):
        return jnp.logical_and(carry[0] >= 0, carry[1] > EXP2_UNDERFLOW)

    def body(carry):
        j, _, c = carry
        start = pl.multiple_of(j * tk, tk)
        w, r = weights(qs, start, tk, c, tq, False)
        for p in range(hp):
            acc_ref[p] += _dot(w[2 * p * tq:2 * (p + 1) * tq], values(start, tk, p))
        c = c + r
        return j - 1, jnp.max(c), c

    lax.while_loop(cond, body, (i - 1, jnp.max(c), c))
    for p in range(hp):
        o_ref[:, p * LANES:(p + 1) * LANES] = jnp.where(
            lane < SB_HEAD_DIM, acc_ref[p, 0:tq], acc_ref[p, tq:2 * tq]).astype(o_ref.dtype)


def _sb_attention(qkv, batch, seq, *, tq=256, hp=4):
    t = batch * seq
    groups = SB_HEADS // 2 // hp
    nq = seq // tq
    w = hp * LANES
    return pl.pallas_call(
        functools.partial(_sb_attn_body, tq=tq, hp=hp),
        grid=(batch, groups, nq),
        in_specs=[
            pl.BlockSpec((tq, w), lambda b, g, i: (b * nq + i, g)),
            pl.BlockSpec((seq, w), lambda b, g, i: (b, groups + g)),
            pl.BlockSpec((seq, w), lambda b, g, i: (b, 2 * groups + g)),
        ],
        out_specs=pl.BlockSpec((tq, w), lambda b, g, i: (b * nq + i, g)),
        out_shape=jax.ShapeDtypeStruct((t, groups * w), BF16),
        scratch_shapes=[pltpu.VMEM((hp, 2 * tq, LANES), F32)],
        compiler_params=_params("parallel", "parallel", "arbitrary"),
        name="sb_attention",
    )(qkv, qkv, qkv)


def _mlstm_body(qk_ref, v_ref, op_ref, gcol_ref, grow_ref, cw_ref, bcol_ref, brow_ref, hg_ref,
                o_ref, xs_ref, st_ref, m_ref, *, chunk):
    L = chunk
    c = pl.program_id(1)
    dqk = ML_HEADS * ML_QK_DIM

    @pl.when(c == 0)
    def _():
        xs_ref[0:8, :] = jnp.zeros((8, 2 * dqk), F32)
        st_ref[...] = jnp.zeros_like(st_ref)
        m_ref[...] = jnp.zeros_like(m_ref)

    xs_ref[8:L + 8, :] = qk_ref[...].astype(F32)
    y = cw_ref[0:1, :] * xs_ref[5:L + 5, :]
    for tap in range(1, ML_CONV):
        y = y + cw_ref[tap:tap + 1, :] * xs_ref[5 + tap:L + 5 + tap, :]
    xs_ref[0:8, :] = xs_ref[L:L + 8, :]
    y = y * jax.nn.sigmoid(y)
    q_all = (y[:, :dqk] * (ML_QK_DIM ** -0.5)).astype(BF16)
    k_all = y[:, dqk:].astype(BF16)

    gcol = gcol_ref[...] + bcol_ref[...]
    ti = lax.broadcasted_iota(jnp.int32, (L, L), 0)
    si = lax.broadcasted_iota(jnp.int32, (L, L), 1)
    causal = ti >= si
    tri = causal.astype(BF16)
    bcol = _split_dot_left(tri, _log_sigmoid(gcol))
    grow = grow_ref[0] + brow_ref[...]
    brow = _split_dot(_log_sigmoid(grow), (si >= ti).astype(BF16))

    lane = lax.broadcasted_iota(jnp.int32, (1, LANES), 1)
    srow = lax.broadcasted_iota(jnp.int32, (LANES, 1), 0)
    ones_v = jnp.ones((L, ML_V_DIM), F32)

    for p in range(ML_HEADS // 2):
        qp = q_all[:, p * LANES:(p + 1) * LANES]
        kp = k_all[:, p * LANES:(p + 1) * LANES]
        state = st_ref[p]
        state_b = state.astype(BF16)
        upd = []
        dec = []
        for hh in range(2):
            h = 2 * p + hh
            sel = (lane >= ML_QK_DIM) if hh else (lane < ML_QK_DIM)
            qm = jnp.where(sel, qp, jnp.zeros_like(qp))
            km = jnp.where(sel, kp, jnp.zeros_like(kp))
            vh = v_ref[:, h * ML_V_DIM:(h + 1) * ML_V_DIM]
            b_c = bcol[:, ML_HEADS + h:ML_HEADS + h + 1]
            li_c = gcol[:, h:h + 1]
            b_r = brow[ML_HEADS + h:ML_HEADS + h + 1, :]
            li_r = grow[h:h + 1, :]
            m_prev = m_ref[h:h + 1, 0:1]

            dmat = jnp.where(causal, b_c - b_r + li_r, -jnp.inf)
            inter = b_c + m_prev
            m_t = jnp.maximum(jnp.max(dmat, axis=-1, keepdims=True), inter)
            wt = jnp.exp(dmat - m_t)
            a = jnp.exp(inter - m_t)
            s = _dot_nt(qm, kp) * wt
            qc = _dot(qm, state_b)
            num = _dot(s.astype(BF16), vh) + a * qc[:, :ML_V_DIM]
            den = jnp.sum(s, axis=-1, keepdims=True) + a * qc[:, ML_V_DIM:]
            hout = num / jnp.maximum(jnp.abs(den), jnp.exp(-m_t))

            b_last = b_c[L - 1:L, :]
            g = b_last - b_c + li_c
            m_new = jnp.maximum(b_last + m_prev, jnp.max(g, axis=0, keepdims=True))
            dec.append(jnp.exp(b_last + m_prev - m_new))
            wk = jnp.exp(g - m_new)
            v_aug = jnp.concatenate([vh.astype(F32), ones_v], axis=1)
            upd.append(_dot_tn(km, (wk * v_aug).astype(BF16)))
            m_ref[h:h + 1, :] = jnp.broadcast_to(m_new, (1, LANES))

            ms = jnp.mean(hout * hout, axis=-1, keepdims=True)
            hn = hout * lax.rsqrt(ms + EPS) * hg_ref[:, h * ML_V_DIM:(h + 1) * ML_V_DIM]
            og = jax.nn.sigmoid(op_ref[:, h * ML_V_DIM:(h + 1) * ML_V_DIM].astype(F32))
            o_ref[:, h * ML_V_DIM:(h + 1) * ML_V_DIM] = (og * hn).astype(o_ref.dtype)

        decay = jnp.where(srow < ML_QK_DIM, dec[0], dec[1])
        st_ref[p] = decay * state + upd[0] + upd[1]


def _split_dot_left(m, x):
    hi = x.astype(BF16)
    lo = (x - hi.astype(F32)).astype(BF16)
    return _dot(m, hi) + _dot(m, lo)


def _mlstm(proj, gates, conv_w, b_i, b_f, head_gain, batch, seq, *, chunk=256):
    t = batch * seq
    d = ML_HEADS * ML_V_DIM
    nc = seq // chunk
    grow = gates[:, :2 * ML_HEADS].reshape(batch, seq, 2 * ML_HEADS).transpose(0, 2, 1)
    bias = jnp.concatenate([b_i, b_f]).astype(F32)
    bcol = jnp.zeros((1, LANES), F32).at[0, :2 * ML_HEADS].set(bias)
    brow = bias.reshape(2 * ML_HEADS, 1)
    return pl.pallas_call(
        functools.partial(_mlstm_body, chunk=chunk),
        grid=(batch, nc),
        in_specs=[
            pl.BlockSpec((chunk, d), lambda b, c: (b * nc + c, 0)),
            pl.BlockSpec((chunk, d), lambda b, c: (b * nc + c, 1)),
            pl.BlockSpec((chunk, d), lambda b, c: (b * nc + c, 2)),
            pl.BlockSpec((chunk, LANES), lambda b, c: (b * nc + c, 0)),
            pl.BlockSpec((1, 2 * ML_HEADS, chunk), lambda b, c: (b, 0, c)),
            pl.BlockSpec((ML_CONV, d), lambda b, c: (0, 0)),
            pl.BlockSpec((1, LANES), lambda b, c: (0, 0)),
            pl.BlockSpec((2 * ML_HEADS, 1), lambda b, c: (0, 0)),
            pl.BlockSpec((1, d), lambda b, c: (0, 0)),
        ],
        out_specs=pl.BlockSpec((chunk, d), lambda b, c: (b * nc + c, 0)),
        out_shape=jax.ShapeDtypeStruct((t, d), BF16),
        scratch_shapes=[
            pltpu.VMEM((chunk + 8, d), F32),
            pltpu.VMEM((ML_HEADS // 2, LANES, 2 * ML_V_DIM), F32),
            pltpu.VMEM((ML_HEADS, LANES), F32),
        ],
        compiler_params=_params("parallel", "arbitrary"),
        name="mlstm",
    )(proj, proj, proj, gates, grow, conv_w.astype(F32), bcol, brow, head_gain.reshape(1, d))


def _mla_up_body(h_ref, g_ref, wc_ref, pos_ref, gq_ref, gkv_ref, wq1_ref, wq2_ref, wk_ref, wv_ref, inv_ref,
                 q_ref, k_ref, v_ref):
    qr, kvr = MLA_Q_RANK, MLA_KV_RANK
    x = h_ref[...]
    xn = (x * lax.rsqrt(jnp.mean(x * x, axis=-1, keepdims=True) + EPS) * g_ref[...]).astype(BF16)
    c = _dot(xn, wc_ref[...])
    cq = c[:, 0:qr]
    ckv = c[:, 512:512 + kvr]
    kr1 = c[:, 768:896]
    kr2 = c[:, 896:1024]
    cqn = (cq * lax.rsqrt(jnp.mean(cq * cq, axis=-1, keepdims=True) + EPS) * gq_ref[...]).astype(BF16)
    ckvn = (ckv * lax.rsqrt(jnp.mean(ckv * ckv, axis=-1, keepdims=True) + EPS) * gkv_ref[...]).astype(BF16)

    ang = pos_ref[...].astype(F32) * inv_ref[...]
    cosv = jnp.cos(ang)
    sinv = jnp.sin(ang)
    lane = lax.broadcasted_iota(jnp.int32, (1, LANES), 1)
    half = MLA_ROPE // 2
    cpat = jnp.where(lane < MLA_NOPE, 1.0, jnp.where(lane < MLA_NOPE + MLA_ROPE, cosv, 0.0))
    spat = jnp.where(lane < MLA_NOPE, 0.0,
                     jnp.where(lane < MLA_NOPE + half, -sinv,
                               jnp.where(lane < MLA_NOPE + MLA_ROPE, sinv, 0.0)))
    scale = (MLA_NOPE + MLA_ROPE) ** -0.5 * LOG2_E
    cpat_q = cpat * scale
    spat_q = spat * scale

    a1 = _dot(cqn, wq1_ref[...])
    a2 = _dot(cqn, wq2_ref[...])
    kn = _dot(ckvn, wk_ref[...])
    vv = _dot(ckvn, wv_ref[...])
    rk = kr1 * cpat + kr2 * spat
    ones = [jnp.where(lane >= MLA_V, 1.0, 0.0), jnp.where(lane < MLA_V, 1.0, 0.0)]
    for h in range(MLA_HEADS):
        sl = slice(h * LANES, (h + 1) * LANES)
        q_ref[:, sl] = (a1[:, sl] * cpat_q + a2[:, sl] * spat_q).astype(BF16)
        k_ref[:, sl] = (kn[:, sl] + rk).astype(BF16)
        v_ref[:, sl] = (vv[:, sl] + ones[h % 2]).astype(BF16)


def _mla_up(h, g, w_c, pos, g_q, g_kv, wq1, wq2, wk, wv, inv_pat, *, tm=512):
    t, d = h.shape
    hw = MLA_HEADS * LANES
    full = _resident
    return pl.pallas_call(
        _mla_up_body,
        grid=(t // tm,),
        in_specs=[
            pl.BlockSpec((tm, d), lambda i: (i, 0)),
            full((1, d)), full(w_c.shape),
            pl.BlockSpec((tm, 1), lambda i: (i, 0)),
            full((1, MLA_Q_RANK)), full((1, MLA_KV_RANK)),
            full(wq1.shape), full(wq2.shape), full(wk.shape), full(wv.shape),
            full((1, LANES)),
        ],
        out_specs=[
            pl.BlockSpec((tm, hw), lambda i: (i, 0)),
            pl.BlockSpec((tm, hw), lambda i: (i, 0)),
            pl.BlockSpec((tm, hw), lambda i: (i, 0)),
        ],
        out_shape=[
            jax.ShapeDtypeStruct((t, hw), BF16),
            jax.ShapeDtypeStruct((t, hw), BF16),
            jax.ShapeDtypeStruct((t, hw), BF16),
        ],
        compiler_params=_params("parallel"),
        name="mla_up",
    )(h, g.reshape(1, d), w_c, pos, g_q.reshape(1, -1), g_kv.reshape(1, -1), wq1, wq2, wk, wv, inv_pat)


def _mla_attn_body(q_ref, k_ref, v_ref, o_ref, acc_ref, *, tq, tk, kb, hp):
    i = pl.program_id(2)
    nh = 2 * hp
    nd = tq // tk
    lane = lax.broadcasted_iota(jnp.int32, (1, LANES), 1)
    acc_ref[...] = jnp.zeros_like(acc_ref)
    qs = [q_ref[:, h * LANES:(h + 1) * LANES] for h in range(nh)]

    def run(units, ms):
        ms = list(ms)

        def score(unit):
            h, start, r0 = unit
            return _dot_nt(qs[h][(r0 or 0):], k_ref[pl.ds(start, tk), h * LANES:(h + 1) * LANES])

        ss = [score(u) for u in units[:2]]
        for n, (h, start, r0) in enumerate(units):
            if n + 2 < len(units):
                ss.append(score(units[n + 2]))
            s = ss[n]
            if r0 is not None:
                r = lax.broadcasted_iota(jnp.int32, s.shape, 0)
                cc = lax.broadcasted_iota(jnp.int32, s.shape, 1)
                s = jnp.where(cc <= r, s, -jnp.inf)
            r0 = r0 or 0
            m_old = ms[h][r0:]
            m_new = jnp.maximum(m_old, jnp.max(s, axis=-1, keepdims=True))
            alpha = jnp.exp2(m_old - m_new)
            pr = jnp.exp2(s - m_new)
            vb = v_ref[pl.ds(start, tk), h * LANES:(h + 1) * LANES]
            acc_ref[h, r0:, :] = alpha * acc_ref[h, r0:, :] + _dot(pr.astype(BF16), vb)
            ms[h] = m_new if r0 == 0 else jnp.concatenate([ms[h][:r0], m_new], axis=0)
        return tuple(ms)

    def full_blocks(jb, ms):
        base = jb * (kb * tk)
        return run([(h, pl.multiple_of(base + u * tk, tk), None) for u in range(kb) for h in range(nh)], ms)

    neg = jnp.full((tq, 1), -jnp.inf, F32)
    ms = lax.fori_loop(0, i * (nd // kb), full_blocks, (neg,) * nh)
    base = i * tq
    run([(h, pl.multiple_of(base + d * tk, tk), d * tk) for d in range(nd) for h in range(nh)], ms)
    for p in range(hp):
        r0 = acc_ref[2 * p]
        r1 = acc_ref[2 * p + 1]
        r0 = r0 / pltpu.roll(r0, MLA_V, axis=1)
        r1 = r1 / pltpu.roll(r1, MLA_V, axis=1)
        o_ref[:, p * LANES:(p + 1) * LANES] = jnp.where(lane < MLA_V, r0, r1).astype(o_ref.dtype)


def _mla_attention(q, k, v, batch, seq, *, tq=1024, tk=512, kb=2, hp=1):
    assert tq % (tk * kb) == 0
    t = batch * seq
    groups = MLA_HEADS // 2 // hp
    nq = seq // tq
    w = 2 * hp * LANES
    return pl.pallas_call(
        functools.partial(_mla_attn_body, tq=tq, tk=tk, kb=kb, hp=hp),
        grid=(batch, groups, nq),
        in_specs=[
            pl.BlockSpec((tq, w), lambda b, g, i: (b * nq + i, g)),
            pl.BlockSpec((seq, w), lambda b, g, i: (b, g)),
            pl.BlockSpec((seq, w), lambda b, g, i: (b, g)),
        ],
        out_specs=pl.BlockSpec((tq, hp * LANES), lambda b, g, i: (b * nq + i, g)),
        out_shape=jax.ShapeDtypeStruct((t, groups * hp * LANES), BF16),
        scratch_shapes=[pltpu.VMEM((2 * hp, tq, LANES), F32)],
        compiler_params=_params("parallel", "parallel", "arbitrary"),
        name="mla_attention",
    )(q, k, v)


def _sb_mixer(h, g, w_in, batch, seq):
    d = h.shape[1]
    scale = SB_HEAD_DIM ** -0.5 * LOG2_E
    w = jnp.concatenate([w_in[:, :d] * scale, w_in[:, d:]], axis=1).astype(BF16)
    (qkv,) = _norm_matmul(h, g, [w], [BF16], name="sb_in_proj")
    return _sb_attention(qkv, batch, seq)


def _mlstm_mixer(h, g, w_in, conv_w, b_i, b_f, head_gain, batch, seq):
    d = h.shape[1]
    w_main = w_in[:, :3 * d].astype(BF16)
    w_gate = jnp.zeros((d, LANES), F32).at[:, :2 * ML_HEADS].set(w_in[:, 3 * d:]).astype(BF16)
    proj, gates = _norm_matmul(h, g, [w_main, w_gate], [BF16, F32], name="ml_in_proj")
    return _mlstm(proj, gates, conv_w, b_i, b_f, head_gain, batch, seq)


def _mla_weights(w_in, w_uq, w_ukv):
    d = w_in.shape[0]
    qr, kvr, r = MLA_Q_RANK, MLA_KV_RANK, MLA_ROPE
    half = r // 2
    w_kr = w_in[:, qr + kvr:]
    w_kr_swap = jnp.concatenate([w_kr[:, half:], w_kr[:, :half]], axis=1)
    place = lambda wr: jnp.zeros((d, LANES), F32).at[:, MLA_NOPE:MLA_NOPE + r].set(wr)
    w_c = jnp.concatenate([
        w_in[:, :qr], jnp.zeros((d, 512 - qr), F32),
        w_in[:, qr:qr + kvr], place(w_kr), place(w_kr_swap)], axis=1)

    wq = w_uq.reshape(qr, MLA_HEADS, MLA_NOPE + r)
    q_rope = wq[:, :, MLA_NOPE:]
    q_swap = jnp.concatenate([q_rope[:, :, half:], q_rope[:, :, :half]], axis=2)
    pad = jnp.zeros((qr, MLA_HEADS, LANES - MLA_NOPE - r), F32)
    wq1 = jnp.concatenate([wq, pad], axis=2).reshape(qr, MLA_HEADS * LANES)
    wq2 = jnp.concatenate([jnp.zeros((qr, MLA_HEADS, MLA_NOPE), F32), q_swap, pad], axis=2)
    wq2 = wq2.reshape(qr, MLA_HEADS * LANES)

    wkv = w_ukv.reshape(kvr, MLA_HEADS, MLA_NOPE + MLA_V)
    wk = jnp.concatenate([wkv[:, :, :MLA_NOPE], jnp.zeros((kvr, MLA_HEADS, LANES - MLA_NOPE), F32)], axis=2)
    wk = wk.reshape(kvr, MLA_HEADS * LANES)
    wvp = wkv[:, :, MLA_NOPE:].reshape(kvr, MLA_HEADS // 2, 2, MLA_V)
    zv = jnp.zeros((kvr, MLA_HEADS // 2, MLA_V), F32)
    wv = jnp.stack([jnp.concatenate([wvp[:, :, 0], zv], axis=2),
                    jnp.concatenate([zv, wvp[:, :, 1]], axis=2)], axis=2).reshape(kvr, MLA_HEADS * LANES)
    return w_c.astype(BF16), wq1.astype(BF16), wq2.astype(BF16), wk.astype(BF16), wv.astype(BF16)


def _mla_mixer(h, g, positions, w_in, g_q, w_uq, g_kv, w_ukv, batch, seq):
    w_c, wq1, wq2, wk, wv = _mla_weights(w_in, w_uq, w_ukv)
    half = MLA_ROPE // 2
    inv = ROPE_THETA ** (-jnp.arange(half, dtype=F32) / half)
    inv_pat = jnp.zeros((1, LANES), F32).at[0, MLA_NOPE:MLA_NOPE + MLA_ROPE].set(jnp.concatenate([inv, inv]))
    pos = positions.reshape(batch * seq, 1)
    q, k, v = _mla_up(h, g, w_c, pos, g_q, g_kv, wq1, wq2, wk, wv, inv_pat)
    return _mla_attention(q, k, v, batch, seq)


def kernel(x, positions, ln_ffn1, ffn1_wi, ffn1_wo, ln_mix, ln_ffn2, ffn2_wi, ffn2_wo, sb_w_in, sb_w_out, ml_w_in, ml_conv_w, ml_b_igate, ml_b_fgate, ml_head_gain, ml_w_out, mla_w_in, mla_g_q, mla_w_uq, mla_g_kv, mla_w_ukv, mla_w_out, ln_final):
    batch, seq, d = x.shape
    depth = ln_mix.shape[0]
    h = x.reshape(batch * seq, d)
    wi1, wo1, wi2, wo2 = ffn1_wi, ffn1_wo, ffn2_wi, ffn2_wo
    for i in range(depth):
        h = _ffn(h, ln_ffn1[i], wi1, wo1, i)
        j = i // N_MIXERS
        kind = i % N_MIXERS
        if kind == 0:
            o = _sb_mixer(h, ln_mix[i], sb_w_in[j], batch, seq)
            w_out = sb_w_out[j]
        elif kind == 1:
            o = _mlstm_mixer(h, ln_mix[i], ml_w_in[j], ml_conv_w[j], ml_b_igate[j], ml_b_fgate[j],
                             ml_head_gain[j], batch, seq)
            w_out = ml_w_out[j]
        else:
            o = _mla_mixer(h, ln_mix[i], positions, mla_w_in[j], mla_g_q[j], mla_w_uq[j], mla_g_kv[j],
                           mla_w_ukv[j], batch, seq)
            w_out = mla_w_out[j]
        h = _ffn(h, ln_ffn2[i], wi2, wo2, i,
                 proj=(o, w_out.astype(BF16)), g_final=ln_final if i == depth - 1 else None)
    return h.reshape(batch, seq, d)
```

```python
import functools

import jax
import jax.numpy as jnp
from jax import lax
from jax.experimental import pallas as pl
from jax.experimental.pallas import tpu as pltpu

F32 = jnp.float32
BF16 = jnp.bfloat16

EPS = 1e-6
LANES = 128
VMEM_LIMIT = 56 * 1024 * 1024

SB_HEADS = 16
SB_HEAD_DIM = 64
ML_HEADS = 8
ML_QK_DIM = 64
ML_V_DIM = 128
ML_CONV = 4
MLA_HEADS = 16
MLA_NOPE = 64
MLA_ROPE = 32
MLA_V = 64
MLA_Q_RANK = 384
MLA_KV_RANK = 256
ROPE_THETA = 10000.0
LOG2_E = 1.4426950408889634
N_MIXERS = 3

EXP2_UNDERFLOW = -152.0


def _dot(a, b):
    return jnp.dot(a, b, preferred_element_type=F32)


def _dot_nt(a, b):
    return lax.dot_general(a, b, (((1,), (1,)), ((), ())), preferred_element_type=F32)


def _dot_tn(a, b):
    return lax.dot_general(a, b, (((0,), (0,)), ((), ())), preferred_element_type=F32)


def _split_dot(x, m):
    hi = x.astype(BF16)
    lo = (x - hi.astype(F32)).astype(BF16)
    return _dot(hi, m) + _dot(lo, m)


def _log_sigmoid(x):
    return jnp.minimum(x, 0.0) - jnp.log(1.0 + jnp.exp(-jnp.abs(x)))


def _params(*sem):
    return pltpu.CompilerParams(dimension_semantics=sem, vmem_limit_bytes=VMEM_LIMIT)


def _ffn_body(*refs, tf, fused_proj, final):
    refs = list(refs)
    h_ref = refs.pop(0)
    if fused_proj:
        a_ref, wp_ref = refs.pop(0), refs.pop(0)
    g_ref, wi_ref, wo_ref = refs.pop(0), refs.pop(0), refs.pop(0)
    if final:
        gf_ref = refs.pop(0)
    (o_ref,) = refs
    dff = wo_ref.shape[0]
    nf = dff // tf
    x = h_ref[...]
    if fused_proj:
        x = x + _dot(a_ref[...], wp_ref[...])
    ms = jnp.mean(x * x, axis=-1, keepdims=True)
    xn = (x * lax.rsqrt(ms + EPS) * g_ref[...]).astype(BF16)

    def gate_up(f):
        return (_dot(xn, wi_ref[:, f * tf:(f + 1) * tf].astype(BF16)),
                _dot(xn, wi_ref[:, dff + f * tf:dff + (f + 1) * tf].astype(BF16)))

    acc = None
    nxt = gate_up(0)
    for f in range(nf):
        gate, up = nxt
        if f + 1 < nf:
            nxt = gate_up(f + 1)
        act = (gate * jax.nn.sigmoid(gate) * up).astype(BF16)
        part = _dot(act, wo_ref[f * tf:(f + 1) * tf, :].astype(BF16))
        acc = part if acc is None else acc + part
    y = x + 0.5 * acc
    if final:
        ms = jnp.mean(y * y, axis=-1, keepdims=True)
        y = y * lax.rsqrt(ms + EPS) * gf_ref[...]
    o_ref[...] = y


def _resident(shape):
    return pl.BlockSpec(shape, lambda i: (0,) * len(shape), pipeline_mode=pl.Buffered(1))


def _resident_layer(stacked, layer):
    return pl.BlockSpec((None,) + stacked.shape[1:], lambda i: (layer, 0, 0), pipeline_mode=pl.Buffered(1))


def _ffn(h, g, wi_all, wo_all, layer, *, proj=None, g_final=None, tm=512, tf=256):
    t, d = h.shape
    final = g_final is not None
    rows = lambda width: pl.BlockSpec((tm, width), lambda i: (i, 0))
    in_specs = [rows(d)]
    args = [h]
    if proj is not None:
        a, wp = proj
        in_specs += [rows(a.shape[1]), _resident(wp.shape)]
        args += [a, wp]
    in_specs += [_resident((1, d)), _resident_layer(wi_all, layer), _resident_layer(wo_all, layer)]
    args += [g.reshape(1, d), wi_all, wo_all]
    if final:
        in_specs.append(_resident((1, d)))
        args.append(g_final.reshape(1, d))
    return pl.pallas_call(
        functools.partial(_ffn_body, tf=tf, fused_proj=proj is not None, final=final),
        grid=(t // tm,),
        in_specs=in_specs,
        out_specs=rows(d),
        out_shape=jax.ShapeDtypeStruct((t, d), F32),
        compiler_params=_params("parallel"),
        name="ffn_final" if final else ("ffn_proj" if proj is not None else "ffn"),
    )(*args)


def _norm_matmul_body(x_ref, g_ref, *refs, sub):
    nw = len(refs) // 2
    w_refs, o_refs = refs[:nw], refs[nw:]
    for r in range(x_ref.shape[0] // sub):
        rows = slice(r * sub, (r + 1) * sub)
        x = x_ref[rows, :]
        ms = jnp.mean(x * x, axis=-1, keepdims=True)
        xn = (x * lax.rsqrt(ms + EPS) * g_ref[...]).astype(BF16)
        for w_ref, o_ref in zip(w_refs, o_refs):
            o_ref[rows, :] = _dot(xn, w_ref[...]).astype(o_ref.dtype)


def _norm_matmul(x, g, weights, out_dtypes, *, tm=1024, sub=512, name="norm_matmul"):
    t, k = x.shape
    rows = lambda width: pl.BlockSpec((tm, width), lambda i: (i, 0))
    return pl.pallas_call(
        functools.partial(_norm_matmul_body, sub=sub),
        grid=(t // tm,),
        in_specs=[rows(k), _resident((1, k))] + [_resident(w.shape) for w in weights],
        out_specs=[rows(w.shape[1]) for w in weights],
        out_shape=[jax.ShapeDtypeStruct((t, w.shape[1]), dt) for w, dt in zip(weights, out_dtypes)],
        compiler_params=_params("parallel"),
        name=name,
    )(x, g.reshape(1, k), *weights)


def _sb_attn_body(q_ref, k_ref, v_ref, o_ref, acc_ref, *, tq, hp):
    i = pl.program_id(2)
    tk = tq
    nh = 2 * hp
    lane = lax.broadcasted_iota(jnp.int32, (1, LANES), 1)
    qs = []
    for p in range(hp):
        qp = q_ref[:, p * LANES:(p + 1) * LANES]
        qs.append(jnp.concatenate([jnp.where(lane < SB_HEAD_DIM, qp, jnp.zeros_like(qp)),
                                   jnp.where(lane >= SB_HEAD_DIM, qp, jnp.zeros_like(qp))], axis=0))
    def weights(qst, start, width, c, rows_per_head, masked):
        z = jnp.concatenate(
            [_dot_nt(qst[p], k_ref[pl.ds(start, width), p * LANES:(p + 1) * LANES]) for p in range(hp)],
            axis=0)
        nz = jnp.minimum(z, 0.0)
        pz = z - nz
        soft = jnp.log2(1.0 + jnp.exp2(nz - pz))
        log_beta = nz - soft
        log_fail = log_beta - z
        if masked:
            strict = (lax.broadcasted_iota(jnp.int32, (rows_per_head, width), 1)
                      < lax.broadcasted_iota(jnp.int32, (rows_per_head, width), 0))
            strict = jnp.concatenate([strict] * (z.shape[0] // rows_per_head), axis=0)
            log_fail = jnp.where(strict, log_fail, 0.0)
            log_beta = jnp.where(strict, log_beta, -jnp.inf)
        later = (lax.broadcasted_iota(jnp.int32, (width, width), 0)
                 > lax.broadcasted_iota(jnp.int32, (width, width), 1)).astype(BF16)
        between = _dot(log_fail.astype(BF16), later) + c
        w = jnp.exp2(log_beta + between).astype(BF16)
        return w, jnp.sum(log_fail, axis=-1, keepdims=True)

    def values(start, width, p):
        return v_ref[pl.ds(start, width), p * LANES:(p + 1) * LANES]

    base = pl.multiple_of(i * tq, tq)
    w, c = weights(qs, base, tk, jnp.zeros((nh * tq, 1), F32), tq, True)
    for p in range(hp):
        acc_ref[p] = _dot(w[2 * p * tq:2 * (p + 1) * tq], values(base, tk, p))

    def cond(carry):
        return jnp.logical_and(carry[0] >= 0, carry[1] > EXP2_UNDERFLOW)

    def body(carry):
        j, _, c = carry
        start = pl.multiple_of(j * tk, tk)
        w, r = weights(qs, start, tk, c, tq, False)
        for p in range(hp):
            acc_ref[p] += _dot(w[2 * p * tq:2 * (p + 1) * tq], values(start, tk, p))
        c = c + r
        return j - 1, jnp.max(c), c

    lax.while_loop(cond, body, (i - 1, jnp.max(c), c))
    for p in range(hp):
        o_ref[:, p * LANES:(p + 1) * LANES] = jnp.where(
            lane < SB_HEAD_DIM, acc_ref[p, 0:tq], acc_ref[p, tq:2 * tq]).astype(o_ref.dtype)


def _sb_attention(qkv, batch, seq, *, tq=256, hp=4):
    t = batch * seq
    groups = SB_HEADS // 2 // hp
    nq = seq // tq
    w = hp * LANES
    return pl.pallas_call(
        functools.partial(_sb_attn_body, tq=tq, hp=hp),
        grid=(batch, groups, nq),
        in_specs=[
            pl.BlockSpec((tq, w), lambda b, g, i: (b * nq + i, g)),
            pl.BlockSpec((seq, w), lambda b, g, i: (b, groups + g)),
            pl.BlockSpec((seq, w), lambda b, g, i: (b, 2 * groups + g)),
        ],
        out_specs=pl.BlockSpec((tq, w), lambda b, g, i: (b * nq + i, g)),
        out_shape=jax.ShapeDtypeStruct((t, groups * w), BF16),
        scratch_shapes=[pltpu.VMEM((hp, 2 * tq, LANES), F32)],
        compiler_params=_params("parallel", "parallel", "arbitrary"),
        name="sb_attention",
    )(qkv, qkv, qkv)


def _mlstm_body(qk_ref, v_ref, op_ref, gcol_ref, grow_ref, cw_ref, bcol_ref, brow_ref, hg_ref,
                o_ref, xs_ref, st_ref, m_ref, *, chunk):
    L = chunk
    c = pl.program_id(1)
    dqk = ML_HEADS * ML_QK_DIM

    @pl.when(c == 0)
    def _():
        xs_ref[0:8, :] = jnp.zeros((8, 2 * dqk), F32)
        st_ref[...] = jnp.zeros_like(st_ref)
        m_ref[...] = jnp.zeros_like(m_ref)

    xs_ref[8:L + 8, :] = qk_ref[...].astype(F32)
    y = cw_ref[0:1, :] * xs_ref[5:L + 5, :]
    for tap in range(1, ML_CONV):
        y = y + cw_ref[tap:tap + 1, :] * xs_ref[5 + tap:L + 5 + tap, :]
    xs_ref[0:8, :] = xs_ref[L:L + 8, :]
    y = y * jax.nn.sigmoid(y)
    q_all = (y[:, :dqk] * (ML_QK_DIM ** -0.5)).astype(BF16)
    k_all = y[:, dqk:].astype(BF16)

    gcol = gcol_ref[...] + bcol_ref[...]
    ti = lax.broadcasted_iota(jnp.int32, (L, L), 0)
    si = lax.broadcasted_iota(jnp.int32, (L, L), 1)
    causal = ti >= si
    tri = causal.astype(BF16)
    bcol = _split_dot_left(tri, _log_sigmoid(gcol))
    grow = grow_ref[0] + brow_ref[...]
    brow = _split_dot(_log_sigmoid(grow), (si >= ti).astype(BF16))

    lane = lax.broadcasted_iota(jnp.int32, (1, LANES), 1)
    srow = lax.broadcasted_iota(jnp.int32, (LANES, 1), 0)
    ones_v = jnp.ones((L, ML_V_DIM), F32)

    for p in range(ML_HEADS // 2):
        qp = q_all[:, p * LANES:(p + 1) * LANES]
        kp = k_all[:, p * LANES:(p + 1) * LANES]
        state = st_ref[p]
        state_b = state.astype(BF16)
        upd = []
        dec = []
        for hh in range(2):
            h = 2 * p + hh
            sel = (lane >= ML_QK_DIM) if hh else (lane < ML_QK_DIM)
            qm = jnp.where(sel, qp, jnp.zeros_like(qp))
            km = jnp.where(sel, kp, jnp.zeros_like(kp))
            vh = v_ref[:, h * ML_V_DIM:(h + 1) * ML_V_DIM]
            b_c = bcol[:, ML_HEADS + h:ML_HEADS + h + 1]
            li_c = gcol[:, h:h + 1]
            b_r = brow[ML_HEADS + h:ML_HEADS + h + 1, :]
            li_r = grow[h:h + 1, :]
            m_prev = m_ref[h:h + 1, 0:1]

            dmat = jnp.where(causal, b_c - b_r + li_r, -jnp.inf)
            inter = b_c + m_prev
            m_t = jnp.maximum(jnp.max(dmat, axis=-1, keepdims=True), inter)
            wt = jnp.exp(dmat - m_t)
            a = jnp.exp(inter - m_t)
            s = _dot_nt(qm, kp) * wt
            qc = _dot(qm, state_b)
            num = _dot(s.astype(BF16), vh) + a * qc[:, :ML_V_DIM]
            den = jnp.sum(s, axis=-1, keepdims=True) + a * qc[:, ML_V_DIM:]
            hout = num / jnp.maximum(jnp.abs(den), jnp.exp(-m_t))

            b_last = b_c[L - 1:L, :]
            g = b_last - b_c + li_c
            m_new = jnp.maximum(b_last + m_prev, jnp.max(g, axis=0, keepdims=True))
            dec.append(jnp.exp(b_last + m_prev - m_new))
            wk = jnp.exp(g - m_new)
            v_aug = jnp.concatenate([vh.astype(F32), ones_v], axis=1)
            upd.append(_dot_tn(km, (wk * v_aug).astype(BF16)))
            m_ref[h:h + 1, :] = jnp.broadcast_to(m_new, (1, LANES))

            ms = jnp.mean(hout * hout, axis=-1, keepdims=True)
            hn = hout * lax.rsqrt(ms + EPS) * hg_ref[:, h * ML_V_DIM:(h + 1) * ML_V_DIM]
            og = jax.nn.sigmoid(op_ref[:, h * ML_V_DIM:(h + 1) * ML_V_DIM].astype(F32))
            o_ref[:, h * ML_V_DIM:(h + 1) * ML_V_DIM] = (og * hn).astype(o_ref.dtype)

        decay = jnp.where(srow < ML_QK_DIM, dec[0], dec[1])
        st_ref[p] = decay * state + upd[0] + upd[1]


def _split_dot_left(m, x):
    hi = x.astype(BF16)
    lo = (x - hi.astype(F32)).astype(BF16)
    return _dot(m, hi) + _dot(m, lo)


def _mlstm(proj, gates, conv_w, b_i, b_f, head_gain, batch, seq, *, chunk=256):
    t = batch * seq
    d = ML_HEADS * ML_V_DIM
    nc = seq // chunk
    grow = gates[:, :2 * ML_HEADS].reshape(batch, seq, 2 * ML_HEADS).transpose(0, 2, 1)
    bias = jnp.concatenate([b_i, b_f]).astype(F32)
    bcol = jnp.zeros((1, LANES), F32).at[0, :2 * ML_HEADS].set(bias)
    brow = bias.reshape(2 * ML_HEADS, 1)
    return pl.pallas_call(
        functools.partial(_mlstm_body, chunk=chunk),
        grid=(batch, nc),
        in_specs=[
            pl.BlockSpec((chunk, d), lambda b, c: (b * nc + c, 0)),
            pl.BlockSpec((chunk, d), lambda b, c: (b * nc + c, 1)),
            pl.BlockSpec((chunk, d), lambda b, c: (b * nc + c, 2)),
            pl.BlockSpec((chunk, LANES), lambda b, c: (b * nc + c, 0)),
            pl.BlockSpec((1, 2 * ML_HEADS, chunk), lambda b, c: (b, 0, c)),
            pl.BlockSpec((ML_CONV, d), lambda b, c: (0, 0)),
            pl.BlockSpec((1, LANES), lambda b, c: (0, 0)),
            pl.BlockSpec((2 * ML_HEADS, 1), lambda b, c: (0, 0)),
            pl.BlockSpec((1, d), lambda b, c: (0, 0)),
        ],
        out_specs=pl.BlockSpec((chunk, d), lambda b, c: (b * nc + c, 0)),
        out_shape=jax.ShapeDtypeStruct((t, d), BF16),
        scratch_shapes=[
            pltpu.VMEM((chunk + 8, d), F32),
            pltpu.VMEM((ML_HEADS // 2, LANES, 2 * ML_V_DIM), F32),
            pltpu.VMEM((ML_HEADS, LANES), F32),
        ],
        compiler_params=_params("parallel", "arbitrary"),
        name="mlstm",
    )(proj, proj, proj, gates, grow, conv_w.astype(F32), bcol, brow, head_gain.reshape(1, d))


def _mla_up_body(h_ref, g_ref, wc_ref, pos_ref, gq_ref, gkv_ref, wq1_ref, wq2_ref, wk_ref, wv_ref, inv_ref,
                 q_ref, k_ref, v_ref):
    qr, kvr = MLA_Q_RANK, MLA_KV_RANK
    x = h_ref[...]
    xn = (x * lax.rsqrt(jnp.mean(x * x, axis=-1, keepdims=True) + EPS) * g_ref[...]).astype(BF16)
    c = _dot(xn, wc_ref[...])
    cq = c[:, 0:qr]
    ckv = c[:, 512:512 + kvr]
    kr1 = c[:, 768:896]
    kr2 = c[:, 896:1024]
    cqn = (cq * lax.rsqrt(jnp.mean(cq * cq, axis=-1, keepdims=True) + EPS) * gq_ref[...]).astype(BF16)
    ckvn = (ckv * lax.rsqrt(jnp.mean(ckv * ckv, axis=-1, keepdims=True) + EPS) * gkv_ref[...]).astype(BF16)

    ang = pos_ref[...].astype(F32) * inv_ref[...]
    cosv = jnp.cos(ang)
    sinv = jnp.sin(ang)
    lane = lax.broadcasted_iota(jnp.int32, (1, LANES), 1)
    half = MLA_ROPE // 2
    cpat = jnp.where(lane < MLA_NOPE, 1.0, jnp.where(lane < MLA_NOPE + MLA_ROPE, cosv, 0.0))
    spat = jnp.where(lane < MLA_NOPE, 0.0,
                     jnp.where(lane < MLA_NOPE + half, -sinv,
                               jnp.where(lane < MLA_NOPE + MLA_ROPE, sinv, 0.0)))
    scale = (MLA_NOPE + MLA_ROPE) ** -0.5 * LOG2_E
    cpat_q = cpat * scale
    spat_q = spat * scale

    a1 = _dot(cqn, wq1_ref[...])
    a2 = _dot(cqn, wq2_ref[...])
    kn = _dot(ckvn, wk_ref[...])
    vv = _dot(ckvn, wv_ref[...])
    rk = kr1 * cpat + kr2 * spat
    ones = [jnp.where(lane >= MLA_V, 1.0, 0.0), jnp.where(lane < MLA_V, 1.0, 0.0)]
    for h in range(MLA_HEADS):
        sl = slice(h * LANES, (h + 1) * LANES)
        q_ref[:, sl] = (a1[:, sl] * cpat_q + a2[:, sl] * spat_q).astype(BF16)
        k_ref[:, sl] = (kn[:, sl] + rk).astype(BF16)
        v_ref[:, sl] = (vv[:, sl] + ones[h % 2]).astype(BF16)


def _mla_up(h, g, w_c, pos, g_q, g_kv, wq1, wq2, wk, wv, inv_pat, *, tm=512):
    t, d = h.shape
    hw = MLA_HEADS * LANES
    full = _resident
    return pl.pallas_call(
        _mla_up_body,
        grid=(t // tm,),
        in_specs=[
            pl.BlockSpec((tm, d), lambda i: (i, 0)),
            full((1, d)), full(w_c.shape),
            pl.BlockSpec((tm, 1), lambda i: (i, 0)),
            full((1, MLA_Q_RANK)), full((1, MLA_KV_RANK)),
            full(wq1.shape), full(wq2.shape), full(wk.shape), full(wv.shape),
            full((1, LANES)),
        ],
        out_specs=[
            pl.BlockSpec((tm, hw), lambda i: (i, 0)),
            pl.BlockSpec((tm, hw), lambda i: (i, 0)),
            pl.BlockSpec((tm, hw), lambda i: (i, 0)),
        ],
        out_shape=[
            jax.ShapeDtypeStruct((t, hw), BF16),
            jax.ShapeDtypeStruct((t, hw), BF16),
            jax.ShapeDtypeStruct((t, hw), BF16),
        ],
        compiler_params=_params("parallel"),
        name="mla_up",
    )(h, g.reshape(1, d), w_c, pos, g_q.reshape(1, -1), g_kv.reshape(1, -1), wq1, wq2, wk, wv, inv_pat)


def _mla_attn_body(q_ref, k_ref, v_ref, o_ref, acc_ref, *, tq, tk, kb, hp):
    i = pl.program_id(2)
    nh = 2 * hp
    nd = tq // tk
    lane = lax.broadcasted_iota(jnp.int32, (1, LANES), 1)
    acc_ref[...] = jnp.zeros_like(acc_ref)
    qs = [q_ref[:, h * LANES:(h + 1) * LANES] for h in range(nh)]

    def run(units, ms):
        ms = list(ms)

        def score(unit):
            h, start, r0 = unit
            return _dot_nt(qs[h][(r0 or 0):], k_ref[pl.ds(start, tk), h * LANES:(h + 1) * LANES])

        ss = [score(u) for u in units[:2]]
        for n, (h, start, r0) in enumerate(units):
            if n + 2 < len(units):
                ss.append(score(units[n + 2]))
            s = ss[n]
            if r0 is not None:
                r = lax.broadcasted_iota(jnp.int32, s.shape, 0)
                cc = lax.broadcasted_iota(jnp.int32, s.shape, 1)
                s = jnp.where(cc <= r, s, -jnp.inf)
            r0 = r0 or 0
            m_old = ms[h][r0:]
            m_new = jnp.maximum(m_old, jnp.max(s, axis=-1, keepdims=True))
            alpha = jnp.exp2(m_old - m_new)
            pr = jnp.exp2(s - m_new)
            vb = v_ref[pl.ds(start, tk), h * LANES:(h + 1) * LANES]
            acc_ref[h, r0:, :] = alpha * acc_ref[h, r0:, :] + _dot(pr.astype(BF16), vb)
            ms[h] = m_new if r0 == 0 else jnp.concatenate([ms[h][:r0], m_new], axis=0)
        return tuple(ms)

    def full_blocks(jb, ms):
        base = jb * (kb * tk)
        return run([(h, pl.multiple_of(base + u * tk, tk), None) for u in range(kb) for h in range(nh)], ms)

    neg = jnp.full((tq, 1), -jnp.inf, F32)
    ms = lax.fori_loop(0, i * (nd // kb), full_blocks, (neg,) * nh)
    base = i * tq
    run([(h, pl.multiple_of(base + d * tk, tk), d * tk) for d in range(nd) for h in range(nh)], ms)
    for p in range(hp):
        r0 = acc_ref[2 * p]
        r1 = acc_ref[2 * p + 1]
        r0 = r0 / pltpu.roll(r0, MLA_V, axis=1)
        r1 = r1 / pltpu.roll(r1, MLA_V, axis=1)
        o_ref[:, p * LANES:(p + 1) * LANES] = jnp.where(lane < MLA_V, r0, r1).astype(o_ref.dtype)


def _mla_attention(q, k, v, batch, seq, *, tq=1024, tk=512, kb=2, hp=1):
    assert tq % (tk * kb) == 0
    t = batch * seq
    groups = MLA_HEADS // 2 // hp
    nq = seq // tq
    w = 2 * hp * LANES
    return pl.pallas_call(
        functools.partial(_mla_attn_body, tq=tq, tk=tk, kb=kb, hp=hp),
        grid=(batch, groups, nq),
        in_specs=[
            pl.BlockSpec((tq, w), lambda b, g, i: (b * nq + i, g)),
            pl.BlockSpec((seq, w), lambda b, g, i: (b, g)),
            pl.BlockSpec((seq, w), lambda b, g, i: (b, g)),
        ],
        out_specs=pl.BlockSpec((tq, hp * LANES), lambda b, g, i: (b * nq + i, g)),
        out_shape=jax.ShapeDtypeStruct((t, groups * hp * LANES), BF16),
        scratch_shapes=[pltpu.VMEM((2 * hp, tq, LANES), F32)],
        compiler_params=_params("parallel", "parallel", "arbitrary"),
        name="mla_attention",
    )(q, k, v)


def _sb_mixer(h, g, w_in, batch, seq):
    d = h.shape[1]
    scale = SB_HEAD_DIM ** -0.5 * LOG2_E
    w = jnp.concatenate([w_in[:, :d] * scale, w_in[:, d:]], axis=1).astype(BF16)
    (qkv,) = _norm_matmul(h, g, [w], [BF16], name="sb_in_proj")
    return _sb_attention(qkv, batch, seq)


def _mlstm_mixer(h, g, w_in, conv_w, b_i, b_f, head_gain, batch, seq):
    d = h.shape[1]
    w_main = w_in[:, :3 * d].astype(BF16)
    w_gate = jnp.zeros((d, LANES), F32).at[:, :2 * ML_HEADS].set(w_in[:, 3 * d:]).astype(BF16)
    proj, gates = _norm_matmul(h, g, [w_main, w_gate], [BF16, F32], name="ml_in_proj")
    return _mlstm(proj, gates, conv_w, b_i, b_f, head_gain, batch, seq)


def _mla_weights(w_in, w_uq, w_ukv):
    d = w_in.shape[0]
    qr, kvr, r = MLA_Q_RANK, MLA_KV_RANK, MLA_ROPE
    half = r // 2
    w_kr = w_in[:, qr + kvr:]
    w_kr_swap = jnp.concatenate([w_kr[:, half:], w_kr[:, :half]], axis=1)
    place = lambda wr: jnp.zeros((d, LANES), F32).at[:, MLA_NOPE:MLA_NOPE + r].set(wr)
    w_c = jnp.concatenate([
        w_in[:, :qr], jnp.zeros((d, 512 - qr), F32),
        w_in[:, qr:qr + kvr], place(w_kr), place(w_kr_swap)], axis=1)

    wq = w_uq.reshape(qr, MLA_HEADS, MLA_NOPE + r)
    q_rope = wq[:, :, MLA_NOPE:]
    q_swap = jnp.concatenate([q_rope[:, :, half:], q_rope[:, :, :half]], axis=2)
    pad = jnp.zeros((qr, MLA_HEADS, LANES - MLA_NOPE - r), F32)
    wq1 = jnp.concatenate([wq, pad], axis=2).reshape(qr, MLA_HEADS * LANES)
    wq2 = jnp.concatenate([jnp.zeros((qr, MLA_HEADS, MLA_NOPE), F32), q_swap, pad], axis=2)
    wq2 = wq2.reshape(qr, MLA_HEADS * LANES)

    wkv = w_ukv.reshape(kvr, MLA_HEADS, MLA_NOPE + MLA_V)
    wk = jnp.concatenate([wkv[:, :, :MLA_NOPE], jnp.zeros((kvr, MLA_HEADS, LANES - MLA_NOPE), F32)], axis=2)
    wk = wk.reshape(kvr, MLA_HEADS * LANES)
    wvp = wkv[:, :, MLA_NOPE:].reshape(kvr, MLA_HEADS // 2, 2, MLA_V)
    zv = jnp.zeros((kvr, MLA_HEADS // 2, MLA_V), F32)
    wv = jnp.stack([jnp.concatenate([wvp[:, :, 0], zv], axis=2),
                    jnp.concatenate([zv, wvp[:, :, 1]], axis=2)], axis=2).reshape(kvr, MLA_HEADS * LANES)
    return w_c.astype(BF16), wq1.astype(BF16), wq2.astype(BF16), wk.astype(BF16), wv.astype(BF16)


def _mla_mixer(h, g, positions, w_in, g_q, w_uq, g_kv, w_ukv, batch, seq):
    w_c, wq1, wq2, wk, wv = _mla_weights(w_in, w_uq, w_ukv)
    half = MLA_ROPE // 2
    inv = ROPE_THETA ** (-jnp.arange(half, dtype=F32) / half)
    inv_pat = jnp.zeros((1, LANES), F32).at[0, MLA_NOPE:MLA_NOPE + MLA_ROPE].set(jnp.concatenate([inv, inv]))
    pos = positions.reshape(batch * seq, 1)
    q, k, v = _mla_up(h, g, w_c, pos, g_q, g_kv, wq1, wq2, wk, wv, inv_pat)
    return _mla_attention(q, k, v, batch, seq)


def kernel(x, positions, ln_ffn1, ffn1_wi, ffn1_wo, ln_mix, ln_ffn2, ffn2_wi, ffn2_wo, sb_w_in, sb_w_out, ml_w_in, ml_conv_w, ml_b_igate, ml_b_fgate, ml_head_gain, ml_w_out, mla_w_in, mla_g_q, mla_w_uq, mla_g_kv, mla_w_ukv, mla_w_out, ln_final):
    batch, seq, d = x.shape
    depth = ln_mix.shape[0]
    h = x.reshape(batch * seq, d)
    wi1, wo1, wi2, wo2 = ffn1_wi, ffn1_wo, ffn2_wi, ffn2_wo
    for i in range(depth):
        h = _ffn(h, ln_ffn1[i], wi1, wo1, i)
        j = i // N_MIXERS
        kind = i % N_MIXERS
        if kind == 0:
            o = _sb_mixer(h, ln_mix[i], sb_w_in[j], batch, seq)
            w_out = sb_w_out[j]
        elif kind == 1:
            o = _mlstm_mixer(h, ln_mix[i], ml_w_in[j], ml_conv_w[j], ml_b_igate[j], ml_b_fgate[j],
                             ml_head_gain[j], batch, seq)
            w_out = ml_w_out[j]
        else:
            o = _mla_mixer(h, ln_mix[i], positions, mla_w_in[j], mla_g_q[j], mla_w_uq[j], mla_g_kv[j],
                           mla_w_ukv[j], batch, seq)
            w_out = mla_w_out[j]
        h = _ffn(h, ln_ffn2[i], wi2, wo2, i,
                 proj=(o, w_out.astype(BF16)), g_final=ln_final if i == depth - 1 else None)
    return h.reshape(batch, seq, d)
```

```python
import functools

import jax
import jax.numpy as jnp
from jax import lax
from jax.experimental import pallas as pl
from jax.experimental.pallas import tpu as pltpu

F32 = jnp.float32
BF16 = jnp.bfloat16

EPS = 1e-6
LANES = 128
VMEM_LIMIT = 56 * 1024 * 1024

SB_HEADS = 16
SB_HEAD_DIM = 64
ML_HEADS = 8
ML_QK_DIM = 64
ML_V_DIM = 128
ML_CONV = 4
MLA_HEADS = 16
MLA_NOPE = 64
MLA_ROPE = 32
MLA_V = 64
MLA_Q_RANK = 384
MLA_KV_RANK = 256
ROPE_THETA = 10000.0
LOG2_E = 1.4426950408889634
N_MIXERS = 3

EXP2_UNDERFLOW = -152.0


def _dot(a, b):
    return jnp.dot(a, b, preferred_element_type=F32)


def _dot_nt(a, b):
    return lax.dot_general(a, b, (((1,), (1,)), ((), ())), preferred_element_type=F32)


def _dot_tn(a, b):
    return lax.dot_general(a, b, (((0,), (0,)), ((), ())), preferred_element_type=F32)


def _split_dot(x, m):
    hi = x.astype(BF16)
    lo = (x - hi.astype(F32)).astype(BF16)
    return _dot(hi, m) + _dot(lo, m)


def _log_sigmoid(x):
    return jnp.minimum(x, 0.0) - jnp.log(1.0 + jnp.exp(-jnp.abs(x)))


def _params(*sem):
    return pltpu.CompilerParams(dimension_semantics=sem, vmem_limit_bytes=VMEM_LIMIT)


def _ffn_body(*refs, tf, fused_proj, final):
    refs = list(refs)
    h_ref = refs.pop(0)
    if fused_proj:
        a_ref, wp_ref = refs.pop(0), refs.pop(0)
    g_ref, wi_ref, wo_ref = refs.pop(0), refs.pop(0), refs.pop(0)
    if final:
        gf_ref = refs.pop(0)
    (o_ref,) = refs
    dff = wo_ref.shape[0]
    nf = dff // tf
    x = h_ref[...]
    if fused_proj:
        x = x + _dot(a_ref[...], wp_ref[...])
    ms = jnp.mean(x * x, axis=-1, keepdims=True)
    xn = (x * lax.rsqrt(ms + EPS) * g_ref[...]).astype(BF16)

    def gate_up(f):
        return (_dot(xn, wi_ref[:, f * tf:(f + 1) * tf].astype(BF16)),
                _dot(xn, wi_ref[:, dff + f * tf:dff + (f + 1) * tf].astype(BF16)))

    acc = None
    nxt = gate_up(0)
    for f in range(nf):
        gate, up = nxt
        if f + 1 < nf:
            nxt = gate_up(f + 1)
        act = (gate * jax.nn.sigmoid(gate) * up).astype(BF16)
        part = _dot(act, wo_ref[f * tf:(f + 1) * tf, :].astype(BF16))
        acc = part if acc is None else acc + part
    y = x + 0.5 * acc
    if final:
        ms = jnp.mean(y * y, axis=-1, keepdims=True)
        y = y * lax.rsqrt(ms + EPS) * gf_ref[...]
    o_ref[...] = y


def _resident(shape):
    return pl.BlockSpec(shape, lambda i: (0,) * len(shape), pipeline_mode=pl.Buffered(1))


def _resident_layer(stacked, layer):
    return pl.BlockSpec((None,) + stacked.shape[1:], lambda i: (layer, 0, 0), pipeline_mode=pl.Buffered(1))


def _ffn(h, g, wi_all, wo_all, layer, *, proj=None, g_final=None, tm=512, tf=256):
    t, d = h.shape
    final = g_final is not None
    rows = lambda width: pl.BlockSpec((tm, width), lambda i: (i, 0))
    in_specs = [rows(d)]
    args = [h]
    if proj is not None:
        a, wp = proj
        in_specs += [rows(a.shape[1]), _resident(wp.shape)]
        args += [a, wp]
    in_specs += [_resident((1, d)), _resident_layer(wi_all, layer), _resident_layer(wo_all, layer)]
    args += [g.reshape(1, d), wi_all, wo_all]
    if final:
        in_specs.append(_resident((1, d)))
        args.append(g_final.reshape(1, d))
    return pl.pallas_call(
        functools.partial(_ffn_body, tf=tf, fused_proj=proj is not None, final=final),
        grid=(t // tm,),
        in_specs=in_specs,
        out_specs=rows(d),
        out_shape=jax.ShapeDtypeStruct((t, d), F32),
        compiler_params=_params("parallel"),
        name="ffn_final" if final else ("ffn_proj" if proj is not None else "ffn"),
    )(*args)


def _norm_matmul_body(x_ref, g_ref, *refs, sub):
    nw = len(refs) // 2
    w_refs, o_refs = refs[:nw], refs[nw:]
    for r in range(x_ref.shape[0] // sub):
        rows = slice(r * sub, (r + 1) * sub)
        x = x_ref[rows, :]
        ms = jnp.mean(x * x, axis=-1, keepdims=True)
        xn = (x * lax.rsqrt(ms + EPS) * g_ref[...]).astype(BF16)
        for w_ref, o_ref in zip(w_refs, o_refs):
            o_ref[rows, :] = _dot(xn, w_ref[...]).astype(o_ref.dtype)


def _norm_matmul(x, g, weights, out_dtypes, *, tm=1024, sub=512, name="norm_matmul"):
    t, k = x.shape
    rows = lambda width: pl.BlockSpec((tm, width), lambda i: (i, 0))
    return pl.pallas_call(
        functools.partial(_norm_matmul_body, sub=sub),
        grid=(t // tm,),
        in_specs=[rows(k), _resident((1, k))] + [_resident(w.shape) for w in weights],
        out_specs=[rows(w.shape[1]) for w in weights],
        out_shape=[jax.ShapeDtypeStruct((t, w.shape[1]), dt) for w, dt in zip(weights, out_dtypes)],
        compiler_params=_params("parallel"),
        name=name,
    )(x, g.reshape(1, k), *weights)


def _sb_attn_body(q_ref, k_ref, v_ref, o_ref, acc_ref, *, tq, hp):
    i = pl.program_id(2)
    tk = tq
    nh = 2 * hp
    lane = lax.broadcasted_iota(jnp.int32, (1, LANES), 1)
    qs = []
    for p in range(hp):
        qp = q_ref[:, p * LANES:(p + 1) * LANES]
        qs.append(jnp.concatenate([jnp.where(lane < SB_HEAD_DIM, qp, jnp.zeros_like(qp)),
                                   jnp.where(lane >= SB_HEAD_DIM, qp, jnp.zeros_like(qp))], axis=0))
    def weights(qst, start, width, c, rows_per_head, masked):
        z = jnp.concatenate(
            [_dot_nt(qst[p], k_ref[pl.ds(start, width), p * LANES:(p + 1) * LANES]) for p in range(hp)],
            axis=0)
        nz = jnp.minimum(z, 0.0)
        pz = z - nz
        soft = jnp.log2(1.0 + jnp.exp2(nz - pz))
        log_beta = nz - soft
        log_fail = log_beta - z
        if masked:
            strict = (lax.broadcasted_iota(jnp.int32, (rows_per_head, width), 1)
                      < lax.broadcasted_iota(jnp.int32, (rows_per_head, width), 0))
            strict = jnp.concatenate([strict] * (z.shape[0] // rows_per_head), axis=0)
            log_fail = jnp.where(strict, log_fail, 0.0)
            log_beta = jnp.where(strict, log_beta, -jnp.inf)
        later = (lax.broadcasted_iota(jnp.int32, (width, width), 0)
                 > lax.broadcasted_iota(jnp.int32, (width, width), 1)).astype(BF16)
        between = _dot(log_fail.astype(BF16), later) + c
        w = jnp.exp2(log_beta + between).astype(BF16)
        return w, jnp.sum(log_fail, axis=-1, keepdims=True)

    def values(start, width, p):
        return v_ref[pl.ds(start, width), p * LANES:(p + 1) * LANES]

    hq = tq // 2
    base = pl.multiple_of(i * tq, tq)
    q_late = [jnp.concatenate([qs[p][hq:tq], qs[p][tq + hq:2 * tq]], axis=0) for p in range(hp)]
    w_a, r_a = weights(q_late, base + hq, hq, jnp.zeros((nh * hq, 1), F32), hq, True)
    zero_rows = jnp.zeros((hq, LANES), F32)
    for p in range(hp):
        pv = _dot(w_a[2 * p * hq:2 * (p + 1) * hq], values(base + hq, hq, p))
        acc_ref[p] = jnp.concatenate([zero_rows, pv[0:hq], zero_rows, pv[hq:2 * hq]], axis=0)
    zero_col = jnp.zeros((hq, 1), F32)
    c = jnp.concatenate([piece for h in range(nh) for piece in (zero_col, r_a[h * hq:(h + 1) * hq])], axis=0)
    w_b, r_b = weights(qs, base, hq, c, tq, True)
    for p in range(hp):
        acc_ref[p] += _dot(w_b[2 * p * tq:2 * (p + 1) * tq], values(base, hq, p))
    c = c + r_b

    def cond(carry):
        return jnp.logical_and(carry[0] >= 0, carry[1] > EXP2_UNDERFLOW)

    def body(carry):
        j, _, c = carry
        start = pl.multiple_of(j * tk, tk)
        w, r = weights(qs, start, tk, c, tq, False)
        for p in range(hp):
            acc_ref[p] += _dot(w[2 * p * tq:2 * (p + 1) * tq], values(start, tk, p))
        c = c + r
        return j - 1, jnp.max(c), c

    lax.while_loop(cond, body, (i - 1, jnp.max(c), c))
    for p in range(hp):
        o_ref[:, p * LANES:(p + 1) * LANES] = jnp.where(
            lane < SB_HEAD_DIM, acc_ref[p, 0:tq], acc_ref[p, tq:2 * tq]).astype(o_ref.dtype)


def _sb_attention(qkv, batch, seq, *, tq=256, hp=4):
    t = batch * seq
    groups = SB_HEADS // 2 // hp
    nq = seq // tq
    w = hp * LANES
    return pl.pallas_call(
        functools.partial(_sb_attn_body, tq=tq, hp=hp),
        grid=(batch, groups, nq),
        in_specs=[
            pl.BlockSpec((tq, w), lambda b, g, i: (b * nq + i, g)),
            pl.BlockSpec((seq, w), lambda b, g, i: (b, groups + g)),
            pl.BlockSpec((seq, w), lambda b, g, i: (b, 2 * groups + g)),
        ],
        out_specs=pl.BlockSpec((tq, w), lambda b, g, i: (b * nq + i, g)),
        out_shape=jax.ShapeDtypeStruct((t, groups * w), BF16),
        scratch_shapes=[pltpu.VMEM((hp, 2 * tq, LANES), F32)],
        compiler_params=_params("parallel", "parallel", "arbitrary"),
        name="sb_attention",
    )(qkv, qkv, qkv)


def _mlstm_body(qk_ref, v_ref, op_ref, gcol_ref, grow_ref, cw_ref, bcol_ref, brow_ref, hg_ref,
                o_ref, xs_ref, st_ref, m_ref, *, chunk):
    L = chunk
    c = pl.program_id(1)
    dqk = ML_HEADS * ML_QK_DIM

    @pl.when(c == 0)
    def _():
        xs_ref[0:8, :] = jnp.zeros((8, 2 * dqk), F32)
        st_ref[...] = jnp.zeros_like(st_ref)
        m_ref[...] = jnp.zeros_like(m_ref)

    xs_ref[8:L + 8, :] = qk_ref[...].astype(F32)
    y = cw_ref[0:1, :] * xs_ref[5:L + 5, :]
    for tap in range(1, ML_CONV):
        y = y + cw_ref[tap:tap + 1, :] * xs_ref[5 + tap:L + 5 + tap, :]
    xs_ref[0:8, :] = xs_ref[L:L + 8, :]
    y = y * jax.nn.sigmoid(y)
    q_all = (y[:, :dqk] * (ML_QK_DIM ** -0.5)).astype(BF16)
    k_all = y[:, dqk:].astype(BF16)

    gcol = gcol_ref[...] + bcol_ref[...]
    ti = lax.broadcasted_iota(jnp.int32, (L, L), 0)
    si = lax.broadcasted_iota(jnp.int32, (L, L), 1)
    causal = ti >= si
    tri = causal.astype(BF16)
    bcol = _split_dot_left(tri, _log_sigmoid(gcol))
    grow = grow_ref[0] + brow_ref[...]
    brow = _split_dot(_log_sigmoid(grow), (si >= ti).astype(BF16))

    lane = lax.broadcasted_iota(jnp.int32, (1, LANES), 1)
    srow = lax.broadcasted_iota(jnp.int32, (LANES, 1), 0)
    ones_v = jnp.ones((L, ML_V_DIM), F32)

    for p in range(ML_HEADS // 2):
        qp = q_all[:, p * LANES:(p + 1) * LANES]
        kp = k_all[:, p * LANES:(p + 1) * LANES]
        state = st_ref[p]
        state_b = state.astype(BF16)
        upd = []
        dec = []
        for hh in range(2):
            h = 2 * p + hh
            sel = (lane >= ML_QK_DIM) if hh else (lane < ML_QK_DIM)
            qm = jnp.where(sel, qp, jnp.zeros_like(qp))
            km = jnp.where(sel, kp, jnp.zeros_like(kp))
            vh = v_ref[:, h * ML_V_DIM:(h + 1) * ML_V_DIM]
            b_c = bcol[:, ML_HEADS + h:ML_HEADS + h + 1]
            li_c = gcol[:, h:h + 1]
            b_r = brow[ML_HEADS + h:ML_HEADS + h + 1, :]
            li_r = grow[h:h + 1, :]
            m_prev = m_ref[h:h + 1, 0:1]

            dmat = jnp.where(causal, b_c - b_r + li_r, -jnp.inf)
            inter = b_c + m_prev
            m_t = jnp.maximum(jnp.max(dmat, axis=-1, keepdims=True), inter)
            wt = jnp.exp(dmat - m_t)
            a = jnp.exp(inter - m_t)
            s = _dot_nt(qm, kp) * wt
            qc = _dot(qm, state_b)
            num = _dot(s.astype(BF16), vh) + a * qc[:, :ML_V_DIM]
            den = jnp.sum(s, axis=-1, keepdims=True) + a * qc[:, ML_V_DIM:]
            hout = num / jnp.maximum(jnp.abs(den), jnp.exp(-m_t))

            b_last = b_c[L - 1:L, :]
            g = b_last - b_c + li_c
            m_new = jnp.maximum(b_last + m_prev, jnp.max(g, axis=0, keepdims=True))
            dec.append(jnp.exp(b_last + m_prev - m_new))
            wk = jnp.exp(g - m_new)
            v_aug = jnp.concatenate([vh.astype(F32), ones_v], axis=1)
            upd.append(_dot_tn(km, (wk * v_aug).astype(BF16)))
            m_ref[h:h + 1, :] = jnp.broadcast_to(m_new, (1, LANES))

            ms = jnp.mean(hout * hout, axis=-1, keepdims=True)
            hn = hout * lax.rsqrt(ms + EPS) * hg_ref[:, h * ML_V_DIM:(h + 1) * ML_V_DIM]
            og = jax.nn.sigmoid(op_ref[:, h * ML_V_DIM:(h + 1) * ML_V_DIM].astype(F32))
            o_ref[:, h * ML_V_DIM:(h + 1) * ML_V_DIM] = (og * hn).astype(o_ref.dtype)

        decay = jnp.where(srow < ML_QK_DIM, dec[0], dec[1])
        st_ref[p] = decay * state + upd[0] + upd[1]


def _split_dot_left(m, x):
    hi = x.astype(BF16)
    lo = (x - hi.astype(F32)).astype(BF16)
    return _dot(m, hi) + _dot(m, lo)


def _mlstm(proj, gates, conv_w, b_i, b_f, head_gain, batch, seq, *, chunk=512):
    t = batch * seq
    d = ML_HEADS * ML_V_DIM
    nc = seq // chunk
    grow = gates[:, :2 * ML_HEADS].reshape(batch, seq, 2 * ML_HEADS).transpose(0, 2, 1)
    bias = jnp.concatenate([b_i, b_f]).astype(F32)
    bcol = jnp.zeros((1, LANES), F32).at[0, :2 * ML_HEADS].set(bias)
    brow = bias.reshape(2 * ML_HEADS, 1)
    return pl.pallas_call(
        functools.partial(_mlstm_body, chunk=chunk),
        grid=(batch, nc),
        in_specs=[
            pl.BlockSpec((chunk, d), lambda b, c: (b * nc + c, 0)),
            pl.BlockSpec((chunk, d), lambda b, c: (b * nc + c, 1)),
            pl.BlockSpec((chunk, d), lambda b, c: (b * nc + c, 2)),
            pl.BlockSpec((chunk, LANES), lambda b, c: (b * nc + c, 0)),
            pl.BlockSpec((1, 2 * ML_HEADS, chunk), lambda b, c: (b, 0, c)),
            pl.BlockSpec((ML_CONV, d), lambda b, c: (0, 0)),
            pl.BlockSpec((1, LANES), lambda b, c: (0, 0)),
            pl.BlockSpec((2 * ML_HEADS, 1), lambda b, c: (0, 0)),
            pl.BlockSpec((1, d), lambda b, c: (0, 0)),
        ],
        out_specs=pl.BlockSpec((chunk, d), lambda b, c: (b * nc + c, 0)),
        out_shape=jax.ShapeDtypeStruct((t, d), BF16),
        scratch_shapes=[
            pltpu.VMEM((chunk + 8, d), F32),
            pltpu.VMEM((ML_HEADS // 2, LANES, 2 * ML_V_DIM), F32),
            pltpu.VMEM((ML_HEADS, LANES), F32),
        ],
        compiler_params=_params("parallel", "arbitrary"),
        name="mlstm",
    )(proj, proj, proj, gates, grow, conv_w.astype(F32), bcol, brow, head_gain.reshape(1, d))


def _mla_up_body(h_ref, g_ref, wc_ref, pos_ref, gq_ref, gkv_ref, wq1_ref, wq2_ref, wk_ref, wv_ref, inv_ref,
                 q_ref, k_ref, v_ref):
    qr, kvr = MLA_Q_RANK, MLA_KV_RANK
    x = h_ref[...]
    xn = (x * lax.rsqrt(jnp.mean(x * x, axis=-1, keepdims=True) + EPS) * g_ref[...]).astype(BF16)
    c = _dot(xn, wc_ref[...])
    cq = c[:, 0:qr]
    ckv = c[:, 512:512 + kvr]
    kr1 = c[:, 768:896]
    kr2 = c[:, 896:1024]
    cqn = (cq * lax.rsqrt(jnp.mean(cq * cq, axis=-1, keepdims=True) + EPS) * gq_ref[...]).astype(BF16)
    ckvn = (ckv * lax.rsqrt(jnp.mean(ckv * ckv, axis=-1, keepdims=True) + EPS) * gkv_ref[...]).astype(BF16)

    ang = pos_ref[...].astype(F32) * inv_ref[...]
    cos4 = jnp.cos(ang)
    sin4 = jnp.sin(ang)
    groups = LANES // MLA_ROPE
    unpack = lambda t: jnp.concatenate(
        [pltpu.roll(t, (MLA_NOPE - MLA_ROPE * j) % LANES, axis=1) for j in range(groups)], axis=0)
    cosv = unpack(cos4)
    sinv = unpack(sin4)
    lane = lax.broadcasted_iota(jnp.int32, (1, LANES), 1)
    half = MLA_ROPE // 2
    cpat = jnp.where(lane < MLA_NOPE, 1.0, jnp.where(lane < MLA_NOPE + MLA_ROPE, cosv, 0.0))
    spat = jnp.where(lane < MLA_NOPE, 0.0,
                     jnp.where(lane < MLA_NOPE + half, -sinv,
                               jnp.where(lane < MLA_NOPE + MLA_ROPE, sinv, 0.0)))
    scale = (MLA_NOPE + MLA_ROPE) ** -0.5 * LOG2_E
    cpat_q = cpat * scale
    spat_q = spat * scale

    a1 = _dot(cqn, wq1_ref[...])
    a2 = _dot(cqn, wq2_ref[...])
    kn = _dot(ckvn, wk_ref[...])
    vv = _dot(ckvn, wv_ref[...])
    rk = kr1 * cpat + kr2 * spat
    ones = [jnp.where(lane >= MLA_V, 1.0, 0.0), jnp.where(lane < MLA_V, 1.0, 0.0)]
    for h in range(MLA_HEADS):
        sl = slice(h * LANES, (h + 1) * LANES)
        q_ref[:, sl] = (a1[:, sl] * cpat_q + a2[:, sl] * spat_q).astype(BF16)
        k_ref[:, sl] = (kn[:, sl] + rk).astype(BF16)
        v_ref[:, sl] = (vv[:, sl] + ones[h % 2]).astype(BF16)


def _mla_up(h, g, w_c, positions, g_q, g_kv, wq1, wq2, wk, wv, *, tm=512):
    t, d = h.shape
    hw = MLA_HEADS * LANES
    full = _resident
    groups = LANES // MLA_ROPE
    half = MLA_ROPE // 2
    pos = positions.reshape(t // tm, groups, tm // groups).transpose(0, 2, 1)
    pos = jnp.repeat(pos, MLA_ROPE, axis=2).reshape(t // groups, LANES)
    inv = ROPE_THETA ** (-jnp.arange(half, dtype=F32) / half)
    inv_pat = jnp.tile(jnp.concatenate([inv, inv]), groups).reshape(1, LANES)
    return pl.pallas_call(
        _mla_up_body,
        grid=(t // tm,),
        in_specs=[
            pl.BlockSpec((tm, d), lambda i: (i, 0)),
            full((1, d)), full(w_c.shape),
            pl.BlockSpec((tm // groups, LANES), lambda i: (i, 0)),
            full((1, MLA_Q_RANK)), full((1, MLA_KV_RANK)),
            full(wq1.shape), full(wq2.shape), full(wk.shape), full(wv.shape),
            full((1, LANES)),
        ],
        out_specs=[
            pl.BlockSpec((tm, hw), lambda i: (i, 0)),
            pl.BlockSpec((tm, hw), lambda i: (i, 0)),
            pl.BlockSpec((tm, hw), lambda i: (i, 0)),
        ],
        out_shape=[
            jax.ShapeDtypeStruct((t, hw), BF16),
            jax.ShapeDtypeStruct((t, hw), BF16),
            jax.ShapeDtypeStruct((t, hw), BF16),
        ],
        compiler_params=_params("parallel"),
        name="mla_up",
    )(h, g.reshape(1, d), w_c, pos, g_q.reshape(1, -1), g_kv.reshape(1, -1), wq1, wq2, wk, wv, inv_pat)


def _mla_attn_body(q_ref, k_ref, v_ref, o_ref, acc_ref, *, tq, tk, kb, hp):
    i = pl.program_id(2)
    nh = 2 * hp
    nd = tq // tk
    lane = lax.broadcasted_iota(jnp.int32, (1, LANES), 1)
    acc_ref[...] = jnp.zeros_like(acc_ref)
    qs = [q_ref[:, h * LANES:(h + 1) * LANES] for h in range(nh)]

    def run(units, ms):
        ms = list(ms)

        def score(unit):
            h, start, r0 = unit
            return _dot_nt(qs[h][(r0 or 0):], k_ref[pl.ds(start, tk), h * LANES:(h + 1) * LANES])

        ss = [score(u) for u in units[:2]]
        for n, (h, start, r0) in enumerate(units):
            if n + 2 < len(units):
                ss.append(score(units[n + 2]))
            s = ss[n]
            if r0 is not None:
                r = lax.broadcasted_iota(jnp.int32, s.shape, 0)
                cc = lax.broadcasted_iota(jnp.int32, s.shape, 1)
                s = jnp.where(cc <= r, s, -jnp.inf)
            r0 = r0 or 0
            m_old = ms[h][r0:]
            m_new = jnp.maximum(m_old, jnp.max(s, axis=-1, keepdims=True))
            alpha = jnp.exp2(m_old - m_new)
            pr = jnp.exp2(s - m_new)
            vb = v_ref[pl.ds(start, tk), h * LANES:(h + 1) * LANES]
            acc_ref[h, r0:, :] = alpha * acc_ref[h, r0:, :] + _dot(pr.astype(BF16), vb)
            ms[h] = m_new if r0 == 0 else jnp.concatenate([ms[h][:r0], m_new], axis=0)
        return tuple(ms)

    def full_blocks(jb, ms):
        base = jb * (kb * tk)
        return run([(h, pl.multiple_of(base + u * tk, tk), None) for u in range(kb) for h in range(nh)], ms)

    neg = jnp.full((tq, 1), -jnp.inf, F32)
    ms = lax.fori_loop(0, i * (nd // kb), full_blocks, (neg,) * nh)
    base = i * tq
    run([(h, pl.multiple_of(base + d * tk, tk), d * tk) for d in range(nd) for h in range(nh)], ms)
    for p in range(hp):
        r0 = acc_ref[2 * p]
        r1 = acc_ref[2 * p + 1]
        r0 = r0 / pltpu.roll(r0, MLA_V, axis=1)
        r1 = r1 / pltpu.roll(r1, MLA_V, axis=1)
        o_ref[:, p * LANES:(p + 1) * LANES] = jnp.where(lane < MLA_V, r0, r1).astype(o_ref.dtype)


def _mla_attention(q, k, v, batch, seq, *, tq=1024, tk=512, kb=2, hp=1):
    assert tq % (tk * kb) == 0
    t = batch * seq
    groups = MLA_HEADS // 2 // hp
    nq = seq // tq
    w = 2 * hp * LANES
    return pl.pallas_call(
        functools.partial(_mla_attn_body, tq=tq, tk=tk, kb=kb, hp=hp),
        grid=(batch, groups, nq),
        in_specs=[
            pl.BlockSpec((tq, w), lambda b, g, i: (b * nq + i, g)),
            pl.BlockSpec((seq, w), lambda b, g, i: (b, g)),
            pl.BlockSpec((seq, w), lambda b, g, i: (b, g)),
        ],
        out_specs=pl.BlockSpec((tq, hp * LANES), lambda b, g, i: (b * nq + i, g)),
        out_shape=jax.ShapeDtypeStruct((t, groups * hp * LANES), BF16),
        scratch_shapes=[pltpu.VMEM((2 * hp, tq, LANES), F32)],
        compiler_params=_params("parallel", "parallel", "arbitrary"),
        name="mla_attention",
    )(q, k, v)


def _sb_mixer(h, g, w_in, batch, seq):
    d = h.shape[1]
    scale = SB_HEAD_DIM ** -0.5 * LOG2_E
    w = jnp.concatenate([w_in[:, :d] * scale, w_in[:, d:]], axis=1).astype(BF16)
    (qkv,) = _norm_matmul(h, g, [w], [BF16], name="sb_in_proj")
    return _sb_attention(qkv, batch, seq)


def _mlstm_mixer(h, g, w_in, conv_w, b_i, b_f, head_gain, batch, seq):
    d = h.shape[1]
    w_main = w_in[:, :3 * d].astype(BF16)
    w_gate = jnp.zeros((d, LANES), F32).at[:, :2 * ML_HEADS].set(w_in[:, 3 * d:]).astype(BF16)
    proj, gates = _norm_matmul(h, g, [w_main, w_gate], [BF16, F32], name="ml_in_proj")
    return _mlstm(proj, gates, conv_w, b_i, b_f, head_gain, batch, seq)


def _mla_weights(w_in, w_uq, w_ukv):
    d = w_in.shape[0]
    qr, kvr, r = MLA_Q_RANK, MLA_KV_RANK, MLA_ROPE
    half = r // 2
    w_kr = w_in[:, qr + kvr:]
    w_kr_swap = jnp.concatenate([w_kr[:, half:], w_kr[:, :half]], axis=1)
    place = lambda wr: jnp.zeros((d, LANES), F32).at[:, MLA_NOPE:MLA_NOPE + r].set(wr)
    w_c = jnp.concatenate([
        w_in[:, :qr], jnp.zeros((d, 512 - qr), F32),
        w_in[:, qr:qr + kvr], place(w_kr), place(w_kr_swap)], axis=1)

    wq = w_uq.reshape(qr, MLA_HEADS, MLA_NOPE + r)
    q_rope = wq[:, :, MLA_NOPE:]
    q_swap = jnp.concatenate([q_rope[:, :, half:], q_rope[:, :, :half]], axis=2)
    pad = jnp.zeros((qr, MLA_HEADS, LANES - MLA_NOPE - r), F32)
    wq1 = jnp.concatenate([wq, pad], axis=2).reshape(qr, MLA_HEADS * LANES)
    wq2 = jnp.concatenate([jnp.zeros((qr, MLA_HEADS, MLA_NOPE), F32), q_swap, pad], axis=2)
    wq2 = wq2.reshape(qr, MLA_HEADS * LANES)

    wkv = w_ukv.reshape(kvr, MLA_HEADS, MLA_NOPE + MLA_V)
    wk = jnp.concatenate([wkv[:, :, :MLA_NOPE], jnp.zeros((kvr, MLA_HEADS, LANES - MLA_NOPE), F32)], axis=2)
    wk = wk.reshape(kvr, MLA_HEADS * LANES)
    wvp = wkv[:, :, MLA_NOPE:].reshape(kvr, MLA_HEADS // 2, 2, MLA_V)
    zv = jnp.zeros((kvr, MLA_HEADS // 2, MLA_V), F32)
    wv = jnp.stack([jnp.concatenate([wvp[:, :, 0], zv], axis=2),
                    jnp.concatenate([zv, wvp[:, :, 1]], axis=2)], axis=2).reshape(kvr, MLA_HEADS * LANES)
    return w_c.astype(BF16), wq1.astype(BF16), wq2.astype(BF16), wk.astype(BF16), wv.astype(BF16)


def _mla_mixer(h, g, positions, w_in, g_q, w_uq, g_kv, w_ukv, batch, seq):
    w_c, wq1, wq2, wk, wv = _mla_weights(w_in, w_uq, w_ukv)
    q, k, v = _mla_up(h, g, w_c, positions, g_q, g_kv, wq1, wq2, wk, wv)
    return _mla_attention(q, k, v, batch, seq)


def kernel(x, positions, ln_ffn1, ffn1_wi, ffn1_wo, ln_mix, ln_ffn2, ffn2_wi, ffn2_wo, sb_w_in, sb_w_out, ml_w_in, ml_conv_w, ml_b_igate, ml_b_fgate, ml_head_gain, ml_w_out, mla_w_in, mla_g_q, mla_w_uq, mla_g_kv, mla_w_ukv, mla_w_out, ln_final):
    batch, seq, d = x.shape
    depth = ln_mix.shape[0]
    h = x.reshape(batch * seq, d)
    wi1, wo1, wi2, wo2 = ffn1_wi, ffn1_wo, ffn2_wi, ffn2_wo
    for i in range(depth):
        h = _ffn(h, ln_ffn1[i], wi1, wo1, i)
        j = i // N_MIXERS
        kind = i % N_MIXERS
        if kind == 0:
            o = _sb_mixer(h, ln_mix[i], sb_w_in[j], batch, seq)
            w_out = sb_w_out[j]
        elif kind == 1:
            o = _mlstm_mixer(h, ln_mix[i], ml_w_in[j], ml_conv_w[j], ml_b_igate[j], ml_b_fgate[j],
                             ml_head_gain[j], batch, seq)
            w_out = ml_w_out[j]
        else:
            o = _mla_mixer(h, ln_mix[i], positions, mla_w_in[j], mla_g_q[j], mla_w_uq[j], mla_g_kv[j],
                           mla_w_ukv[j], batch, seq)
            w_out = mla_w_out[j]
        h = _ffn(h, ln_ffn2[i], wi2, wo2, i,
                 proj=(o, w_out.astype(BF16)), g_final=ln_final if i == depth - 1 else None)
    return h.reshape(batch, seq, d)
```
